```python
import jax
import jax.numpy as jnp
from jax import lax
import numpy as np


D_MODEL = 1024
BATCH = 16
SEQ = 4096
DEPTH = 4

CHUNK = 64
N_META = 16
N_HEADS_A = 8
HEAD_DIM = 64
D_A = N_HEADS_A * HEAD_DIM
N_IDX_HEADS = 8
IDX_DIM = HEAD_DIM
TOPK_MAX = 256
POOL_WINDOWS = (2, 4, 8, 16)
N_POOL_GROUPS = len(POOL_WINDOWS)
D_B = D_MODEL // 2
POOL_GROUP_DIM = D_B // N_POOL_GROUPS
Q_BLOCK = 128
ROPE_THETA = 10000.0
EPS = 1e-6
NEG_INF = -1e30
IN_WIDTHS = (D_A, HEAD_DIM, HEAD_DIM, D_A, D_B, D_B, N_IDX_HEADS * IDX_DIM, IDX_DIM, N_IDX_HEADS, 2 * D_MODEL)
D_IN = sum(IN_WIDTHS)

kernel_name = 'hybrid_dsa_pool_gated_block'


def rms_norm(x, g):
    xf = x.astype(jnp.float32)
    y = xf * lax.rsqrt(jnp.mean(xf * xf, axis=-1, keepdims=True) + EPS)
    return (y * g.astype(jnp.float32)).astype(x.dtype)


def rope_tables(length, dim):
    inv_freq = 1.0 / (ROPE_THETA ** (jnp.arange(0, dim, 2, dtype=jnp.float32) / dim))
    ang = jnp.arange(length, dtype=jnp.float32)[:, None] * inv_freq[None, :]
    ang = jnp.concatenate([ang, ang], axis=-1)
    return jnp.cos(ang), jnp.sin(ang)


def apply_rope(x, cos, sin):
    half = x.shape[-1] // 2
    xf = x.astype(jnp.float32)
    rot = jnp.concatenate([-xf[..., half:], xf[..., :half]], axis=-1)
    return (xf * cos + rot * sin).astype(x.dtype)


def chunk_id(pos):
    return jnp.where(pos < N_META, 0, 1 + (pos - N_META) // CHUNK)


def indexer_sparse_attention(q, k, v, q_idx, k_idx, w_idx, k_top):
    B, T = q.shape[0], q.shape[1]
    n_blk = -(-T // Q_BLOCK)
    t_pad = n_blk * Q_BLOCK

    def to_blocks(a):
        a = jnp.pad(a, [(0, 0), (0, t_pad - T)] + [(0, 0)] * (a.ndim - 2))
        return jnp.moveaxis(a.reshape((B, n_blk, Q_BLOCK) + a.shape[2:]), 1, 0)

    key_chunk = chunk_id(jnp.arange(T))
    query_chunk = chunk_id(jnp.arange(t_pad)).reshape(n_blk, Q_BLOCK)
    k_idx_f = k_idx.astype(jnp.float32)
    idx_scale = (N_IDX_HEADS ** -0.5) * (IDX_DIM ** -0.5)
    attn_scale = HEAD_DIM ** -0.5
    gather = jax.vmap(lambda table, ids: table[ids])

    def one_block(args):
        qb, qib, wib, qcb = args
        s = jnp.einsum('bqhd,bkd->bqhk', qib.astype(jnp.float32), k_idx_f)
        score = jnp.einsum('bqhk,bqh->bqk', jax.nn.relu(s), wib.astype(jnp.float32) * idx_scale)
        visible = key_chunk[None, :] <= qcb[:, None]
        score = jnp.where(visible[None], score, NEG_INF)
        _, sel = lax.top_k(score, k_top)
        valid = key_chunk[sel] <= qcb[None, :, None]
        kg = gather(k, sel)
        vg = gather(v, sel)
        logits = jnp.einsum('bqhd,bqkd->bqhk', qb, kg).astype(jnp.float32) * attn_scale
        logits = jnp.where(valid[:, :, None, :], logits, NEG_INF)
        p = jax.nn.softmax(logits, axis=-1).astype(vg.dtype)
        return jnp.einsum('bqhk,bqkd->bqhd', p, vg)

    out = lax.map(one_block, (to_blocks(q), to_blocks(q_idx), to_blocks(w_idx), query_chunk))
    return jnp.moveaxis(out, 0, 1).reshape(B, t_pad, N_HEADS_A, HEAD_DIM)[:, :T]


def multiscale_pool(u, pool_w, pool_b, pool_s):
    B, T, C = u.shape
    G = POOL_GROUP_DIM
    uf = u.astype(jnp.float32)
    cs = jnp.concatenate([jnp.zeros((B, 1, C), jnp.float32), jnp.cumsum(uf, axis=1)], axis=1)
    t1 = jnp.arange(1, T + 1, dtype=jnp.float32)
    means = []
    for g, w in enumerate(POOL_WINDOWS):
        csg = cs[..., g * G:(g + 1) * G]
        lag = jnp.concatenate([jnp.zeros((B, w, G), jnp.float32), csg[:, :T + 1 - w]], axis=1)
        cnt = jnp.minimum(t1, float(w))
        means.append((csg[:, 1:] - lag[:, 1:]) / cnt[None, :, None])
    pooled = (jnp.concatenate(means, axis=-1) - uf).astype(u.dtype).reshape(B, T, N_POOL_GROUPS, G)
    mixed = jnp.einsum('btgc,gcd->btgd', pooled, pool_w) + pool_b.reshape(N_POOL_GROUPS, G)
    return mixed.reshape(B, T, C) * pool_s


def hybrid_layer(x, norm_g, w_in, qn_g, kn_g, pool_w, pool_b, pool_s, w_a, w_b, w_out, cos, sin, k_top):
    B, T, _ = x.shape
    h = rms_norm(x, norm_g)
    proj = jnp.einsum('btd,de->bte', h, w_in)
    q, k, v, gate_a, u_b, gate_b, q_idx, k_idx, w_idx, merge = jnp.split(
        proj, np.cumsum(IN_WIDTHS)[:-1].tolist(), axis=-1)
    cos_h, sin_h = cos[:, None, :], sin[:, None, :]
    q = apply_rope(rms_norm(q.reshape(B, T, N_HEADS_A, HEAD_DIM), qn_g), cos_h, sin_h)
    k = apply_rope(rms_norm(k, kn_g), cos, sin)
    q_idx = apply_rope(q_idx.reshape(B, T, N_IDX_HEADS, IDX_DIM), cos_h, sin_h)
    k_idx = apply_rope(k_idx, cos, sin)
    attn = indexer_sparse_attention(q, k, v, q_idx, k_idx, w_idx, k_top).reshape(B, T, D_A)
    y_a = jnp.einsum('bte,ed->btd', attn * jax.nn.silu(gate_a), w_a)
    pooled = multiscale_pool(u_b, pool_w, pool_b, pool_s)
    y_b = jnp.einsum('bte,ed->btd', pooled * jax.nn.silu(gate_b), w_b)
    g_a, g_b = jnp.split(jax.nn.sigmoid(merge), 2, axis=-1)
    return x + jnp.einsum('btd,de->bte', g_a * y_a + g_b * y_b, w_out)


def setup_inputs(seed: int = 0) -> dict:
    key = jax.random.key(seed)
    ks = jax.random.split(key, 12)

    def nrm(k, shape, scale):
        return scale * jax.random.normal(k, shape, jnp.float32)

    return {
        'x': nrm(ks[0], (BATCH, SEQ, D_MODEL), 1.0),
        'meta_tokens': nrm(ks[1], (N_META, D_MODEL), 1.0),
        'norm_gain': 1.0 + nrm(ks[2], (DEPTH, D_MODEL), 0.05),
        'w_in': nrm(ks[3], (DEPTH, D_MODEL, D_IN), D_MODEL ** -0.5),
        'q_norm_gain': 1.0 + nrm(ks[4], (DEPTH, HEAD_DIM), 0.05),
        'k_norm_gain': 1.0 + nrm(ks[5], (DEPTH, HEAD_DIM), 0.05),
        'pool_w': nrm(ks[6], (DEPTH, N_POOL_GROUPS, POOL_GROUP_DIM, POOL_GROUP_DIM), POOL_GROUP_DIM ** -0.5),
        'pool_b': nrm(ks[7], (DEPTH, D_B), 0.02),
        'pool_scale': 1.0 + nrm(ks[8], (DEPTH, D_B), 0.1),
        'w_branch_a': nrm(ks[9], (DEPTH, D_A, D_MODEL), D_A ** -0.5),
        'w_branch_b': nrm(ks[10], (DEPTH, D_B, D_MODEL), D_B ** -0.5),
        'w_out': nrm(ks[11], (DEPTH, D_MODEL, D_MODEL), D_MODEL ** -0.5),
    }


def reference(x, meta_tokens, norm_gain, w_in, q_norm_gain, k_norm_gain, pool_w, pool_b, pool_scale,
              w_branch_a, w_branch_b, w_out):
    B, S, _ = x.shape
    k_top = min(TOPK_MAX, S // 4)
    meta = jnp.broadcast_to(meta_tokens.astype(x.dtype)[None], (B, N_META, D_MODEL))
    h = jnp.concatenate([meta, x], axis=1)
    T = S + N_META
    cos, sin = rope_tables(T, HEAD_DIM)
    for l in range(DEPTH):
        h = hybrid_layer(h, norm_gain[l], w_in[l], q_norm_gain[l], k_norm_gain[l], pool_w[l], pool_b[l],
                         pool_scale[l], w_branch_a[l], w_branch_b[l], w_out[l], cos, sin, k_top)
    return h[:, N_META:]
```

```python
import functools

import jax
import jax.numpy as jnp
import numpy as np
from jax import lax
from jax.experimental import pallas as pl
from jax.experimental.pallas import tpu as pltpu

D_MODEL = 1024
DEPTH = 4
CHUNK = 64
N_META = 16
N_HEADS = 8
HEAD_DIM = 64
D_A = N_HEADS * HEAD_DIM
D_B = 512
POOL_WINDOWS = (2, 4, 8, 16)
POOL_GROUP = 128
TOPK_MAX = 256
ROPE_THETA = 10000.0
EPS = 1e-6
NEG_INF = -1e30

LANES = 128
PAD = LANES - N_META
KB = LANES
QB = LANES
N_T_ROWS = 2 * D_A + 3 * HEAD_DIM + 16
D_REST = D_A + D_B + D_B + 2 * D_MODEL
INT_MIN = -2 ** 31
VMEM_LIMIT = 48 * 1024 * 1024

MXU_DTYPE = jnp.bfloat16

F32 = jnp.float32
I32 = jnp.int32


def _dot(a, b):
    return jnp.dot(a, b, preferred_element_type=F32)


def _sortable(bits):
    return bits ^ ((bits >> 31) & 0x7FFFFFFF)


_NEG_KEY = int(_sortable(np.float32(NEG_INF).view(np.int32)))


def _proj_kernel(x_ref, g_ref, wt_ref, ws_ref, gq_ref, gk_ref, cos_ref, sin_ref,
                 qt_ref, qit_ref, wit_ref, kk_ref, vt_ref, rest_ref, *, tm):
    x = x_ref[0]
    ms = jnp.mean(x * x, axis=-1, keepdims=True)
    hn = (x * lax.rsqrt(ms + EPS) * g_ref[...]).astype(MXU_DTYPE)

    for c in range(0, D_REST, 512):
        rest_ref[0, :, c:c + 512] = _dot(hn, ws_ref[:, c:c + 512]).astype(rest_ref.dtype)

    pt = lax.dot_general(wt_ref[...], hn, (((1,), (1,)), ((), ())), preferred_element_type=F32)
    cos = cos_ref[...]
    sin = sin_ref[...]
    half = HEAD_DIM // 2

    def rope(t):
        x1, x2 = t[:half], t[half:]
        return jnp.concatenate([x1 * cos - x2 * sin, x2 * cos + x1 * sin], axis=0)

    def norm(t, g):
        return t * lax.rsqrt(jnp.mean(t * t, axis=0, keepdims=True) + EPS) * g

    gq = gq_ref[...]
    for h in range(N_HEADS):
        r0 = h * HEAD_DIM
        qh = rope(norm(pt[r0:r0 + HEAD_DIM], gq)) * (HEAD_DIM ** -0.5)
        qt_ref[0, r0:r0 + HEAD_DIM, :] = qh.astype(qt_ref.dtype)
        qih = rope(pt[D_A + 2 * HEAD_DIM + r0:D_A + 2 * HEAD_DIM + r0 + HEAD_DIM])
        qit_ref[0, r0:r0 + HEAD_DIM, :] = qih.astype(qit_ref.dtype)

    k_t = rope(norm(pt[D_A:D_A + HEAD_DIM], gk_ref[...]))
    v_t = pt[D_A + HEAD_DIM:D_A + 2 * HEAD_DIM]
    r_ki = 2 * D_A + 2 * HEAD_DIM
    ki_t = rope(pt[r_ki:r_ki + HEAD_DIM])
    wi_t = pt[r_ki + HEAD_DIM:r_ki + HEAD_DIM + N_HEADS]
    wit_ref[0] = wi_t * ((N_HEADS ** -0.5) * (HEAD_DIM ** -0.5))

    kk_t = jnp.concatenate([k_t, ki_t], axis=0)
    for c in range(tm // LANES):
        sl = slice(c * LANES, (c + 1) * LANES)
        kk_ref[0, sl, :] = kk_t[:, sl].T.astype(kk_ref.dtype)
        vt_ref[0, c] = v_t[:, sl].astype(vt_ref.dtype)


def _fold8(x):
    parts = [x[r:r + 8] for r in range(0, x.shape[0], 8)]
    while len(parts) > 1:
        parts = [parts[a] + parts[a + 1] for a in range(0, len(parts), 2)]
    return parts[0]


def _attn_kernel(qt_ref, qit_ref, wit_ref, kk_ref, vt_ref, o_ref, keys_scr, acc_scr, *, k_top, n_rows):
    i = pl.program_id(1)
    nb = i + 1
    lane = lax.broadcasted_iota(I32, (KB, QB), 1)
    sub = lax.broadcasted_iota(I32, (KB, QB), 0)
    q_row = i * QB + lane
    vis_end = jnp.where(q_row < LANES, LANES, ((q_row - LANES) // CHUNK + 1) * CHUNK + LANES)
    n_unseen = n_rows - nb * KB

    zeros_half = jnp.zeros((HEAD_DIM, N_HEADS * QB), MXU_DTYPE)

    def stack_heads(ref):
        return jnp.concatenate([ref[0, h * HEAD_DIM:(h + 1) * HEAD_DIM, :] for h in range(N_HEADS)], axis=1)

    q_all = jnp.concatenate([stack_heads(qt_ref), zeros_half], axis=0)
    qi_all = jnp.concatenate([zeros_half, stack_heads(qit_ref)], axis=0)
    w = wit_ref[0]

    def key_rows(j):
        return j * KB + sub

    def visible(j):
        kr = key_rows(j)
        return (kr >= PAD) & (kr < vis_end)

    def score_body(j, carry):
        kkb = kk_ref[0, pl.ds(pl.multiple_of(j * KB, KB), KB), :]
        s = _dot(kkb, qi_all)
        score = jnp.zeros((KB, QB), F32)
        for h in range(N_HEADS):
            score = score + jnp.maximum(s[:, h * QB:(h + 1) * QB], 0.0) * w[h:h + 1, :]
        score = jnp.where(visible(j), score + 0.0, NEG_INF)
        key = _sortable(lax.bitcast_convert_type(score, I32))
        key = jnp.where(key_rows(j) < PAD, INT_MIN, key)
        keys_scr[pl.ds(pl.multiple_of(j * KB, KB), KB), :] = key
        return carry

    lax.fori_loop(0, nb, score_body, 0)

    def count(pred):
        def body(j, acc):
            kb = keys_scr[pl.ds(pl.multiple_of(j * KB, KB), KB), :]
            return acc + _fold8(jnp.where(pred(kb, j), 1, 0))
        acc = lax.fori_loop(0, nb, body, jnp.zeros((8, QB), I32))
        return jnp.sum(acc, axis=0, keepdims=True)

    def radix_body(t, tau_u):
        cand_u = tau_u | jnp.left_shift(jnp.int32(1), 31 - t)
        cand = cand_u ^ INT_MIN
        c = count(lambda kb, j: kb >= cand) + jnp.where(_NEG_KEY >= cand, n_unseen, 0)
        return jnp.where(c >= k_top, cand_u, tau_u)

    tau = lax.fori_loop(0, 32, radix_body, jnp.zeros((1, QB), I32)) ^ INT_MIN
    c_gt = count(lambda kb, j: kb > tau) + jnp.where(_NEG_KEY > tau, n_unseen, 0)
    c_ge = count(lambda kb, j: kb >= tau) + jnp.where(_NEG_KEY >= tau, n_unseen, 0)
    n_tied_wanted = k_top - c_gt

    idx_bits = int(n_rows).bit_length()

    def tie_limit():
        def body(t, lim):
            cand = lim | jnp.left_shift(jnp.int32(1), idx_bits - 1 - t)
            c = count(lambda kb, j: (kb == tau) & (key_rows(j) < cand))
            return jnp.where(c < n_tied_wanted, cand, lim)
        return lax.fori_loop(0, idx_bits, body, jnp.zeros((1, QB), I32))

    all_rows = jnp.full((1, QB), (1 << idx_bits) - 1, I32)
    last_tied_row = lax.cond(jnp.max(c_ge) > k_top, tie_limit, lambda: all_rows)

    acc_scr[...] = jnp.zeros_like(acc_scr)

    def attn_body(j, carry):
        m, l = carry
        off = pl.multiple_of(j * KB, KB)
        kb = keys_scr[pl.ds(off, KB), :]
        chosen = (kb > tau) | ((kb == tau) & (key_rows(j) <= last_tied_row))
        bias = jnp.where(chosen & visible(j), 0.0, NEG_INF)
        s = _dot(kk_ref[0, pl.ds(off, KB), :], q_all)
        s = s + jnp.concatenate([bias] * N_HEADS, axis=1)
        m_new = jnp.maximum(m, jnp.max(s, axis=0, keepdims=True))
        alpha = jnp.exp(m - m_new)
        p = jnp.exp(s - m_new)
        l = alpha * l + jnp.sum(p, axis=0, keepdims=True)
        acc_scr[...] = acc_scr[...] * alpha + _dot(vt_ref[0, j], p.astype(MXU_DTYPE))
        return m_new, l

    m0 = jnp.full((1, N_HEADS * QB), NEG_INF, F32)
    l0 = jnp.zeros((1, N_HEADS * QB), F32)
    _, l = lax.fori_loop(0, nb, attn_body, (m0, l0))

    out_t = acc_scr[...] / l
    for hp in range(N_HEADS // 2):
        blk = jnp.concatenate([out_t[:, (2 * hp) * QB:(2 * hp + 1) * QB],
                               out_t[:, (2 * hp + 1) * QB:(2 * hp + 2) * QB]], axis=0)
        o_ref[0, :, hp * LANES:(hp + 1) * LANES] = blk.T.astype(o_ref.dtype)


def _out_kernel(x_ref, attn_ref, rest_ref, halo_ref, band_ref, wa_ref, pw_ref, pb_ref, ps_ref, wb_ref,
                wo_ref, o_ref, *, tm):
    t = pl.program_id(1)
    row = t * tm + lax.broadcasted_iota(I32, (tm, LANES), 0)
    pos = row - PAD

    gate_a = rest_ref[0, :, 0:D_A].astype(F32)
    a = attn_ref[0].astype(F32) * (gate_a * jax.nn.sigmoid(gate_a))
    y_a = _dot(a.astype(MXU_DTYPE), wa_ref[...])

    u = rest_ref[0, :, D_A:D_A + D_B]
    halo = jnp.where(t > 0, halo_ref[0], jnp.zeros_like(halo_ref[0]))
    u_ext = jnp.concatenate([halo, u], axis=0)
    mixed = []
    for g, win in enumerate(POOL_WINDOWS):
        cs = slice(g * POOL_GROUP, (g + 1) * POOL_GROUP)
        window_sum = _dot(band_ref[g], u_ext[:, cs])
        cnt = jnp.clip(pos + 1, 1, win).astype(F32)
        pooled = window_sum / cnt - u[:, cs].astype(F32)
        mixed.append(_dot(pooled.astype(MXU_DTYPE), pw_ref[g]))
    mixed = (jnp.concatenate(mixed, axis=1) + pb_ref[...]) * ps_ref[...]
    gate_b = rest_ref[0, :, D_A + D_B:D_A + 2 * D_B].astype(F32)
    z_b = mixed * (gate_b * jax.nn.sigmoid(gate_b))
    y_b = _dot(z_b.astype(MXU_DTYPE), wb_ref[...])

    c0 = D_A + 2 * D_B
    g_a = jax.nn.sigmoid(rest_ref[0, :, c0:c0 + D_MODEL].astype(F32))
    g_b = jax.nn.sigmoid(rest_ref[0, :, c0 + D_MODEL:c0 + 2 * D_MODEL].astype(F32))
    mix = g_a * y_a + g_b * y_b
    out = x_ref[0] + _dot(mix.astype(MXU_DTYPE), wo_ref[...])
    row_full = t * tm + lax.broadcasted_iota(I32, (tm, D_MODEL), 0)
    o_ref[0] = jnp.where(row_full >= PAD, out, 0.0)


def _row_tile(n_rows):
    for tm in (384, 256, 128):
        if n_rows % tm == 0:
            return tm
    raise ValueError(f"sequence rows {n_rows} must be a multiple of 128")


def _params(sem):
    return pltpu.CompilerParams(dimension_semantics=sem, vmem_limit_bytes=VMEM_LIMIT)


def _const_spec(shape):
    return pl.BlockSpec(shape, lambda b, t: (0,) * len(shape))


def _layer(h, lw, cos_t, sin_t, band, *, k_top):
    B, n_rows, _ = h.shape
    tm = _row_tile(n_rows)
    nt = n_rows // tm
    nblk = n_rows // LANES
    act = MXU_DTYPE

    qt, qit, wit, kk, vt, rest = pl.pallas_call(
        functools.partial(_proj_kernel, tm=tm),
        grid=(B, nt),
        in_specs=[
            pl.BlockSpec((1, tm, D_MODEL), lambda b, t: (b, t, 0)),
            _const_spec((1, D_MODEL)),
            _const_spec((N_T_ROWS, D_MODEL)),
            _const_spec((D_MODEL, D_REST)),
            _const_spec((HEAD_DIM, tm)),
            _const_spec((HEAD_DIM, tm)),
            pl.BlockSpec((HEAD_DIM // 2, tm), lambda b, t: (0, t)),
            pl.BlockSpec((HEAD_DIM // 2, tm), lambda b, t: (0, t)),
        ],
        out_specs=[
            pl.BlockSpec((1, D_A, tm), lambda b, t: (b, 0, t)),
            pl.BlockSpec((1, D_A, tm), lambda b, t: (b, 0, t)),
            pl.BlockSpec((1, N_HEADS, tm), lambda b, t: (b, 0, t)),
            pl.BlockSpec((1, tm, LANES), lambda b, t: (b, t, 0)),
            pl.BlockSpec((1, tm // LANES, HEAD_DIM, LANES), lambda b, t: (b, t, 0, 0)),
            pl.BlockSpec((1, tm, D_REST), lambda b, t: (b, t, 0)),
        ],
        out_shape=[
            jax.ShapeDtypeStruct((B, D_A, n_rows), act),
            jax.ShapeDtypeStruct((B, D_A, n_rows), act),
            jax.ShapeDtypeStruct((B, N_HEADS, n_rows), F32),
            jax.ShapeDtypeStruct((B, n_rows, LANES), act),
            jax.ShapeDtypeStruct((B, nblk, HEAD_DIM, LANES), act),
            jax.ShapeDtypeStruct((B, n_rows, D_REST), act),
        ],
        compiler_params=_params(("parallel", "parallel")),
        name="proj",
    )(h, lw["g"], lw["wt"], lw["ws"], lw["gq"], lw["gk"], cos_t, sin_t)

    attn = pl.pallas_call(
        functools.partial(_attn_kernel, k_top=k_top, n_rows=n_rows),
        grid=(B, nblk),
        in_specs=[
            pl.BlockSpec((1, D_A, QB), lambda b, i: (b, 0, i)),
            pl.BlockSpec((1, D_A, QB), lambda b, i: (b, 0, i)),
            pl.BlockSpec((1, N_HEADS, QB), lambda b, i: (b, 0, i)),
            pl.BlockSpec((1, n_rows, LANES), lambda b, i: (b, 0, 0)),
            pl.BlockSpec((1, nblk, HEAD_DIM, LANES), lambda b, i: (b, 0, 0, 0)),
        ],
        out_specs=pl.BlockSpec((1, QB, D_A), lambda b, i: (b, i, 0)),
        out_shape=jax.ShapeDtypeStruct((B, n_rows, D_A), act),
        scratch_shapes=[
            pltpu.VMEM((n_rows, QB), I32),
            pltpu.VMEM((HEAD_DIM, N_HEADS * QB), F32),
        ],
        compiler_params=_params(("parallel", "arbitrary")),
        name="attn",
    )(qt, qit, wit, kk, vt)

    halo_blocks = tm // LANES
    out = pl.pallas_call(
        functools.partial(_out_kernel, tm=tm),
        grid=(B, nt),
        in_specs=[
            pl.BlockSpec((1, tm, D_MODEL), lambda b, t: (b, t, 0)),
            pl.BlockSpec((1, tm, D_A), lambda b, t: (b, t, 0)),
            pl.BlockSpec((1, tm, D_REST), lambda b, t: (b, t, 0)),
            pl.BlockSpec((1, LANES, D_B), lambda b, t: (b, jnp.maximum(t * halo_blocks - 1, 0), 1)),
            _const_spec((len(POOL_WINDOWS), tm, LANES + tm)),
            _const_spec((D_A, D_MODEL)),
            _const_spec((len(POOL_WINDOWS), POOL_GROUP, POOL_GROUP)),
            _const_spec((1, D_B)),
            _const_spec((1, D_B)),
            _const_spec((D_B, D_MODEL)),
            _const_spec((D_MODEL, D_MODEL)),
        ],
        out_specs=pl.BlockSpec((1, tm, D_MODEL), lambda b, t: (b, t, 0)),
        out_shape=jax.ShapeDtypeStruct((B, n_rows, D_MODEL), F32),
        compiler_params=_params(("parallel", "parallel")),
        name="outproj",
    )(h, attn, rest, rest, band, lw["wa"], lw["pw"], lw["pb"], lw["ps"], lw["wb"], lw["wo"])
    return out


def _prep_layer_weights(l, tm, norm_gain, w_in, q_norm_gain, k_norm_gain, pool_w, pool_b, pool_scale,
                        w_branch_a, w_branch_b, w_out):
    w = w_in[l]
    o_q, o_k, o_v, o_ga = 0, D_A, D_A + HEAD_DIM, D_A + 2 * HEAD_DIM
    o_ub = o_ga + D_A
    o_gb = o_ub + D_B
    o_qi = o_gb + D_B
    o_ki = o_qi + D_A
    o_wi = o_ki + HEAD_DIM
    o_mg = o_wi + N_HEADS
    w_t = jnp.concatenate([w[:, o_q:o_ga], w[:, o_qi:o_mg],
                           jnp.zeros((D_MODEL, N_T_ROWS - (o_ga - o_q) - (o_mg - o_qi)), w.dtype)], axis=1).T
    w_s = jnp.concatenate([w[:, o_ga:o_qi], w[:, o_mg:]], axis=1)
    return {
        "g": norm_gain[l].reshape(1, D_MODEL).astype(F32),
        "wt": w_t.astype(MXU_DTYPE),
        "ws": w_s.astype(MXU_DTYPE),
        "gq": jnp.broadcast_to(q_norm_gain[l].astype(F32)[:, None], (HEAD_DIM, tm)),
        "gk": jnp.broadcast_to(k_norm_gain[l].astype(F32)[:, None], (HEAD_DIM, tm)),
        "wa": w_branch_a[l].astype(MXU_DTYPE),
        "pw": pool_w[l].astype(MXU_DTYPE),
        "pb": pool_b[l].reshape(1, D_B).astype(F32),
        "ps": pool_scale[l].reshape(1, D_B).astype(F32),
        "wb": w_branch_b[l].astype(MXU_DTYPE),
        "wo": w_out[l].astype(MXU_DTYPE),
    }


@jax.jit
def kernel(x, meta_tokens, norm_gain, w_in, q_norm_gain, k_norm_gain, pool_w, pool_b, pool_scale,
           w_branch_a, w_branch_b, w_out):
    B, S, _ = x.shape
    assert S % LANES == 0, "sequence length must be a multiple of 128"
    k_top = min(TOPK_MAX, S // 4)
    n_rows = LANES + S
    tm = _row_tile(n_rows)

    meta = jnp.broadcast_to(meta_tokens.astype(x.dtype)[None], (B, N_META, D_MODEL))
    h = jnp.concatenate([jnp.zeros((B, PAD, D_MODEL), x.dtype), meta, x], axis=1)

    pos = jnp.maximum(jnp.arange(n_rows, dtype=F32) - PAD, 0.0)
    inv_freq = 1.0 / (ROPE_THETA ** (jnp.arange(0, HEAD_DIM, 2, dtype=F32) / HEAD_DIM))
    ang = inv_freq[:, None] * pos[None, :]
    cos_t, sin_t = jnp.cos(ang), jnp.sin(ang)

    r = jnp.arange(tm)[:, None]
    c = jnp.arange(LANES + tm)[None, :] - LANES
    band = jnp.stack([((c <= r) & (c > r - w)) for w in POOL_WINDOWS]).astype(MXU_DTYPE)

    for l in range(DEPTH):
        lw = _prep_layer_weights(l, tm, norm_gain, w_in, q_norm_gain, k_norm_gain, pool_w, pool_b, pool_scale,
                                 w_branch_a, w_branch_b, w_out)
        h = _layer(h, lw, cos_t, sin_t, band, k_top=k_top)
    return h[:, LANES:]
```

```python
import functools
import math

import jax
import jax.numpy as jnp
import numpy as np
from jax import lax
from jax.experimental import pallas as pl
from jax.experimental.pallas import tpu as pltpu

D_MODEL = 1024
DEPTH = 4
CHUNK = 64
N_META = 16
N_HEADS = 8
HEAD_DIM = 64
D_A = N_HEADS * HEAD_DIM
D_B = 512
POOL_WINDOWS = (2, 4, 8, 16)
POOL_GROUP = 128
TOPK_MAX = 256
ROPE_THETA = 10000.0
EPS = 1e-6
NEG_INF = -1e30

LANES = 128
PAD = LANES - N_META
KB = LANES
QB = LANES
V_ROWS = HEAD_DIM + 16
N_T_ROWS = 2 * D_A + 3 * HEAD_DIM + 16
D_REST = D_A + D_B + D_B + 2 * D_MODEL
INT_MIN = -2 ** 31
INT_MAX = 2 ** 31 - 1
FLOAT_MIDPOINT_STEPS = 24
BISECT_STEPS_PER_TEST = 4
MAX_BISECT_STEPS = FLOAT_MIDPOINT_STEPS + 36
VMEM_LIMIT = 48 * 1024 * 1024

MXU_DTYPE = jnp.bfloat16

F32 = jnp.float32
I32 = jnp.int32


def _dot(a, b):
    return jnp.dot(a, b, preferred_element_type=F32)


def _sortable(bits):
    return bits ^ ((bits >> 31) & 0x7FFFFFFF)


_NEG_KEY = int(_sortable(np.float32(NEG_INF).view(np.int32)))


def _proj_kernel(x_ref, g_ref, wt_ref, ws_ref, gq_ref, gk_ref, cos_ref, sin_ref,
                 qt_ref, qit_ref, wit_ref, kk_ref, vt_ref, rest_ref, *, tm):
    x = x_ref[0]
    ms = jnp.mean(x * x, axis=-1, keepdims=True)
    hn = (x * lax.rsqrt(ms + EPS) * g_ref[...]).astype(MXU_DTYPE)

    for c in range(0, D_REST, 512):
        rest_ref[0, :, c:c + 512] = _dot(hn, ws_ref[:, c:c + 512]).astype(rest_ref.dtype)

    pt = lax.dot_general(wt_ref[...], hn, (((1,), (1,)), ((), ())), preferred_element_type=F32)
    cos = cos_ref[...]
    sin = sin_ref[...]
    half = HEAD_DIM // 2

    def rope(t):
        x1, x2 = t[:half], t[half:]
        return jnp.concatenate([x1 * cos - x2 * sin, x2 * cos + x1 * sin], axis=0)

    def norm(t, g):
        return t * lax.rsqrt(jnp.mean(t * t, axis=0, keepdims=True) + EPS) * g

    gq = gq_ref[...]
    q_scale = (HEAD_DIM ** -0.5) * math.log2(math.e)
    for h in range(N_HEADS):
        r0 = h * HEAD_DIM
        qh = rope(norm(pt[r0:r0 + HEAD_DIM], gq)) * q_scale
        qt_ref[0, r0:r0 + HEAD_DIM, :] = qh.astype(qt_ref.dtype)
        qih = rope(pt[D_A + 2 * HEAD_DIM + r0:D_A + 2 * HEAD_DIM + r0 + HEAD_DIM])
        qit_ref[0, r0:r0 + HEAD_DIM, :] = qih.astype(qit_ref.dtype)

    k_t = rope(norm(pt[D_A:D_A + HEAD_DIM], gk_ref[...]))
    v_t = pt[D_A + HEAD_DIM:D_A + 2 * HEAD_DIM]
    r_ki = 2 * D_A + 2 * HEAD_DIM
    ki_t = rope(pt[r_ki:r_ki + HEAD_DIM])
    wi_t = pt[r_ki + HEAD_DIM:r_ki + HEAD_DIM + N_HEADS]
    wit_ref[0] = wi_t * ((N_HEADS ** -0.5) * (HEAD_DIM ** -0.5))

    kk_t = jnp.concatenate([k_t, ki_t], axis=0)
    for c in range(tm // LANES):
        sl = slice(c * LANES, (c + 1) * LANES)
        kk_ref[0, sl, :] = kk_t[:, sl].T.astype(kk_ref.dtype)
        vt_ref[0, c] = v_t[:, sl].astype(vt_ref.dtype)


def _fold8(x, op):
    parts = [x[r:r + 8] for r in range(0, x.shape[0], 8)]
    while len(parts) > 1:
        nxt = [op(parts[a], parts[a + 1]) for a in range(0, len(parts) - 1, 2)]
        if len(parts) % 2:
            nxt.append(parts[-1])
        parts = nxt
    return parts[0]


def _loop_by_two(n, body, init):
    carry = lax.fori_loop(0, n // 2, lambda t, c: body(2 * t + 1, body(2 * t, c)), init)
    return lax.cond(n % 2 == 1, lambda c: body(n - 1, c), lambda c: c, carry)


def _attn_kernel(qt_ref, qit_ref, wit_ref, kk_ref, vt_ref, o_ref, keys_scr, s_scr, acc_scr, *,
                 k_top, n_rows, ks):
    i = pl.program_id(1)
    nkb = ks // KB
    ns = i // nkb + 1
    n_keys = n_rows - PAD
    n_unseen = n_rows - ns * ks
    lane = lax.broadcasted_iota(I32, (1, QB), 1)
    sub = lax.broadcasted_iota(I32, (KB, QB), 0)
    q_row = i * QB + lane
    vis_end = jnp.where(q_row < LANES, LANES, ((q_row - LANES) // CHUNK + 1) * CHUNK + LANES)
    n_vis = vis_end - PAD

    zeros_half = jnp.zeros((HEAD_DIM, N_HEADS * QB), MXU_DTYPE)

    def stack_heads(ref):
        return jnp.concatenate([ref[0, h * HEAD_DIM:(h + 1) * HEAD_DIM, :] for h in range(N_HEADS)], axis=1)

    q_all = jnp.concatenate([stack_heads(qt_ref), zeros_half], axis=0)
    qi_all = jnp.concatenate([zeros_half, stack_heads(qit_ref)], axis=0)
    w = wit_ref[0]

    def score_body(j, carry):
        kmin, kmax = carry
        base = pl.multiple_of(j * ks, ks)
        kk_sb = kk_ref[0, pl.ds(base, ks), :]
        s = _dot(kk_sb, qi_all)
        score = jnp.maximum(s[:, :QB], 0.0) * w[0:1, :]
        for h in range(1, N_HEADS):
            score = score + jnp.maximum(s[:, h * QB:(h + 1) * QB], 0.0) * w[h:h + 1, :]
        for r in range(nkb):
            off = base + r * KB
            vis = sub < (vis_end - off)
            sc = jnp.where(vis, score[r * KB:(r + 1) * KB] + 0.0, NEG_INF)
            key = _sortable(lax.bitcast_convert_type(sc, I32))
            keys_scr[pl.ds(pl.multiple_of(off, KB), KB), :] = key
            kmax = jnp.maximum(kmax, _fold8(key, jnp.maximum))
            kmin = jnp.minimum(kmin, _fold8(jnp.where(vis, key, INT_MAX), jnp.minimum))
        return kmin, kmax

    kmin, kmax = _loop_by_two(ns, score_body,
                              (jnp.full((8, QB), INT_MAX, I32), jnp.full((8, QB), INT_MIN, I32)))
    keys_scr[0:PAD, :] = jnp.full((PAD, QB), INT_MIN, I32)
    kmin = jnp.min(kmin, axis=0, keepdims=True)
    kmax = jnp.max(kmax, axis=0, keepdims=True)

    @pl.when(ns % 2 == 1)
    def _():
        keys_scr[pl.ds(pl.multiple_of(ns * ks, ks), ks), :] = jnp.full((ks, QB), INT_MIN, I32)

    def count(hit):
        def body(t, acc):
            base = pl.multiple_of(t * (2 * ks), 2 * ks)
            for r in range(2 * nkb):
                off = base + r * KB
                kb = keys_scr[pl.ds(pl.multiple_of(off, KB), KB), :]
                acc = acc + _fold8(hit(kb, off), jnp.add)
            return acc
        acc = lax.fori_loop(0, (ns + 1) // 2, body, jnp.zeros((8, QB), I32))
        return jnp.sum(acc, axis=0, keepdims=True)

    def count_ge(cand):
        c = count(lambda kb, off: jnp.where(kb >= cand, 1, 0))
        return c + jnp.where(_NEG_KEY >= cand, n_unseen, 0)

    few_visible = n_vis < k_top
    lo0 = jnp.where(few_visible, jnp.minimum(kmin, _NEG_KEY), kmin)
    c_lo0 = n_vis + jnp.where(_NEG_KEY >= lo0, n_keys - n_vis, 0)
    top = jnp.maximum(kmax, _NEG_KEY)
    hi0 = jnp.where(top == INT_MAX, INT_MAX, top + 1)

    def finished(lo, hi, c_lo):
        return (c_lo == k_top) | (lo + 1 >= hi)

    def bisect_cond(st):
        p, lo, hi, c_lo, _ = st
        pending = jnp.where(finished(lo, hi, c_lo), 0, 1)
        return (p < MAX_BISECT_STEPS) & (jnp.max(pending) > 0)

    def bisect_body(st):
        for _ in range(BISECT_STEPS_PER_TEST):
            st = bisect_step(st)
        return st

    def bisect_step(st):
        p, lo, hi, c_lo, c_hi = st
        done = finished(lo, hi, c_lo)
        lo_f = lax.bitcast_convert_type(_sortable(lo), F32)
        hi_f = lax.bitcast_convert_type(_sortable(hi), F32)
        cand_f = _sortable(lax.bitcast_convert_type(lo_f * 0.5 + hi_f * 0.5, I32))
        cand_k = (lo >> 1) + (hi >> 1) + (lo & hi & 1)
        use_f = (cand_f > lo) & (cand_f < hi) & (p < FLOAT_MIDPOINT_STEPS)
        cand = jnp.where(use_f, cand_f, cand_k)
        cand = jnp.where(few_visible & (p == 0), _NEG_KEY + 1, cand)
        cand = jnp.minimum(jnp.maximum(cand, lo + 1), hi - 1)
        c = count_ge(cand)
        up = (c >= k_top) & jnp.logical_not(done)
        dn = (c < k_top) & jnp.logical_not(done)
        return (p + 1, jnp.where(up, cand, lo), jnp.where(dn, cand, hi),
                jnp.where(up, c, c_lo), jnp.where(dn, c, c_hi))

    _, tau, _, c_ge, c_gt = lax.while_loop(
        bisect_cond, bisect_body, (jnp.int32(0), lo0, hi0, c_lo0, jnp.zeros((1, QB), I32)))

    idx_bits = int(n_rows).bit_length()
    all_rows = jnp.full((1, QB), (1 << idx_bits) - 1, I32)
    n_tied_wanted = jnp.where(c_ge > k_top, k_top - c_gt, INT_MAX)

    def tie_limit():
        def body(t, lim):
            cand = lim | jnp.left_shift(jnp.int32(1), idx_bits - 1 - t)
            c = count(lambda kb, off: jnp.where(kb == tau, jnp.where(sub < cand - off, 1, 0), 0))
            return jnp.where(c < n_tied_wanted, cand, lim)
        return lax.fori_loop(0, idx_bits, body, jnp.zeros((1, QB), I32))

    last_tied_row = lax.cond(jnp.max(c_ge) > k_top, tie_limit, lambda: all_rows)

    def logits_body(j, m):
        base = pl.multiple_of(j * ks, ks)
        bias = []
        for r in range(nkb):
            off = base + r * KB
            kb = keys_scr[pl.ds(pl.multiple_of(off, KB), KB), :]
            thr = tau - jnp.where(sub <= last_tied_row - off, 1, 0)
            thr = jnp.where(sub < vis_end - off, thr, INT_MAX)
            bias.append(jnp.where(kb > thr, 0.0, NEG_INF))
        bias = jnp.concatenate(bias, axis=0)
        s = _dot(kk_ref[0, pl.ds(base, ks), :], q_all) + jnp.concatenate([bias] * N_HEADS, axis=1)
        s_scr[pl.ds(base, ks), :] = s
        return jnp.maximum(m, jnp.max(s, axis=0, keepdims=True))

    m = _loop_by_two(ns, logits_body, jnp.full((1, N_HEADS * QB), NEG_INF, F32))

    acc_scr[...] = jnp.zeros_like(acc_scr)
    ones_rows = jnp.ones((V_ROWS - HEAD_DIM, ks), MXU_DTYPE)

    def value_body(j, carry):
        base = pl.multiple_of(j * ks, ks)
        p = jnp.exp2(s_scr[pl.ds(base, ks), :] - m).astype(MXU_DTYPE)
        v_ext = jnp.concatenate(
            [jnp.concatenate([vt_ref[0, j * nkb + r] for r in range(nkb)], axis=1), ones_rows], axis=0)
        acc_scr[...] += _dot(v_ext, p)
        return carry

    _loop_by_two(ns, value_body, 0)

    acc = acc_scr[...]
    out_t = acc[:HEAD_DIM] / acc[HEAD_DIM:HEAD_DIM + 1]
    for hp in range(N_HEADS // 2):
        blk = jnp.concatenate([out_t[:, (2 * hp) * QB:(2 * hp + 1) * QB],
                               out_t[:, (2 * hp + 1) * QB:(2 * hp + 2) * QB]], axis=0)
        o_ref[0, :, hp * LANES:(hp + 1) * LANES] = blk.T.astype(o_ref.dtype)


def _out_kernel(x_ref, attn_ref, rest_ref, halo_ref, band_ref, wa_ref, pw_ref, pb_ref, ps_ref, wb_ref,
                wo_ref, o_ref, *, tm):
    t = pl.program_id(1)
    row = t * tm + lax.broadcasted_iota(I32, (tm, LANES), 0)
    pos = row - PAD

    gate_a = rest_ref[0, :, 0:D_A].astype(F32)
    a = attn_ref[0].astype(F32) * (gate_a * jax.nn.sigmoid(gate_a))
    y_a = _dot(a.astype(MXU_DTYPE), wa_ref[...])

    u = rest_ref[0, :, D_A:D_A + D_B]
    halo = jnp.where(t > 0, halo_ref[0], jnp.zeros_like(halo_ref[0]))
    u_ext = jnp.concatenate([halo, u], axis=0)
    mixed = []
    for g, win in enumerate(POOL_WINDOWS):
        cs = slice(g * POOL_GROUP, (g + 1) * POOL_GROUP)
        window_sum = _dot(band_ref[g], u_ext[:, cs])
        cnt = jnp.clip(pos + 1, 1, win).astype(F32)
        pooled = window_sum / cnt - u[:, cs].astype(F32)
        mixed.append(_dot(pooled.astype(MXU_DTYPE), pw_ref[g]))
    mixed = (jnp.concatenate(mixed, axis=1) + pb_ref[...]) * ps_ref[...]
    gate_b = rest_ref[0, :, D_A + D_B:D_A + 2 * D_B].astype(F32)
    z_b = mixed * (gate_b * jax.nn.sigmoid(gate_b))
    y_b = _dot(z_b.astype(MXU_DTYPE), wb_ref[...])

    c0 = D_A + 2 * D_B
    g_a = jax.nn.sigmoid(rest_ref[0, :, c0:c0 + D_MODEL].astype(F32))
    g_b = jax.nn.sigmoid(rest_ref[0, :, c0 + D_MODEL:c0 + 2 * D_MODEL].astype(F32))
    mix = g_a * y_a + g_b * y_b
    out = x_ref[0] + _dot(mix.astype(MXU_DTYPE), wo_ref[...])
    row_full = t * tm + lax.broadcasted_iota(I32, (tm, D_MODEL), 0)
    o_ref[0] = jnp.where(row_full >= PAD, out, 0.0)


def _row_tile(n_rows):
    for tm in (384, 256, 128):
        if n_rows % tm == 0:
            return tm
    raise ValueError(f"sequence rows {n_rows} must be a multiple of 128")


def _key_step(n_rows):
    return 3 * KB if n_rows % (3 * KB) == 0 else KB


def _params(sem):
    return pltpu.CompilerParams(dimension_semantics=sem, vmem_limit_bytes=VMEM_LIMIT)


def _const_spec(shape):
    return pl.BlockSpec(shape, lambda b, t: (0,) * len(shape))


def _layer(h, lw, cos_t, sin_t, band, *, k_top):
    B, n_rows, _ = h.shape
    tm = _row_tile(n_rows)
    nt = n_rows // tm
    nblk = n_rows // LANES
    act = MXU_DTYPE

    qt, qit, wit, kk, vt, rest = pl.pallas_call(
        functools.partial(_proj_kernel, tm=tm),
        grid=(B, nt),
        in_specs=[
            pl.BlockSpec((1, tm, D_MODEL), lambda b, t: (b, t, 0)),
            _const_spec((1, D_MODEL)),
            _const_spec((N_T_ROWS, D_MODEL)),
            _const_spec((D_MODEL, D_REST)),
            _const_spec((HEAD_DIM, tm)),
            _const_spec((HEAD_DIM, tm)),
            pl.BlockSpec((HEAD_DIM // 2, tm), lambda b, t: (0, t)),
            pl.BlockSpec((HEAD_DIM // 2, tm), lambda b, t: (0, t)),
        ],
        out_specs=[
            pl.BlockSpec((1, D_A, tm), lambda b, t: (b, 0, t)),
            pl.BlockSpec((1, D_A, tm), lambda b, t: (b, 0, t)),
            pl.BlockSpec((1, N_HEADS, tm), lambda b, t: (b, 0, t)),
            pl.BlockSpec((1, tm, LANES), lambda b, t: (b, t, 0)),
            pl.BlockSpec((1, tm // LANES, HEAD_DIM, LANES), lambda b, t: (b, t, 0, 0)),
            pl.BlockSpec((1, tm, D_REST), lambda b, t: (b, t, 0)),
        ],
        out_shape=[
            jax.ShapeDtypeStruct((B, D_A, n_rows), act),
            jax.ShapeDtypeStruct((B, D_A, n_rows), act),
            jax.ShapeDtypeStruct((B, N_HEADS, n_rows), F32),
            jax.ShapeDtypeStruct((B, n_rows, LANES), act),
            jax.ShapeDtypeStruct((B, nblk, HEAD_DIM, LANES), act),
            jax.ShapeDtypeStruct((B, n_rows, D_REST), act),
        ],
        compiler_params=_params(("parallel", "parallel")),
        name="proj",
    )(h, lw["g"], lw["wt"], lw["ws"], lw["gq"], lw["gk"], cos_t, sin_t)

    assert n_rows - PAD >= k_top
    ks = _key_step(n_rows)
    attn = pl.pallas_call(
        functools.partial(_attn_kernel, k_top=k_top, n_rows=n_rows, ks=ks),
        grid=(B, nblk),
        in_specs=[
            pl.BlockSpec((1, D_A, QB), lambda b, i: (b, 0, i)),
            pl.BlockSpec((1, D_A, QB), lambda b, i: (b, 0, i)),
            pl.BlockSpec((1, N_HEADS, QB), lambda b, i: (b, 0, i)),
            pl.BlockSpec((1, n_rows, LANES), lambda b, i: (b, 0, 0)),
            pl.BlockSpec((1, nblk, HEAD_DIM, LANES), lambda b, i: (b, 0, 0, 0)),
        ],
        out_specs=pl.BlockSpec((1, QB, D_A), lambda b, i: (b, i, 0)),
        out_shape=jax.ShapeDtypeStruct((B, n_rows, D_A), act),
        scratch_shapes=[
            pltpu.VMEM((n_rows + ks, QB), I32),
            pltpu.VMEM((n_rows, N_HEADS * QB), F32),
            pltpu.VMEM((V_ROWS, N_HEADS * QB), F32),
        ],
        compiler_params=_params(("parallel", "arbitrary")),
        name="attn",
    )(qt, qit, wit, kk, vt)

    halo_blocks = tm // LANES
    out = pl.pallas_call(
        functools.partial(_out_kernel, tm=tm),
        grid=(B, nt),
        in_specs=[
            pl.BlockSpec((1, tm, D_MODEL), lambda b, t: (b, t, 0)),
            pl.BlockSpec((1, tm, D_A), lambda b, t: (b, t, 0)),
            pl.BlockSpec((1, tm, D_REST), lambda b, t: (b, t, 0)),
            pl.BlockSpec((1, LANES, D_B), lambda b, t: (b, jnp.maximum(t * halo_blocks - 1, 0), 1)),
            _const_spec((len(POOL_WINDOWS), tm, LANES + tm)),
            _const_spec((D_A, D_MODEL)),
            _const_spec((len(POOL_WINDOWS), POOL_GROUP, POOL_GROUP)),
            _const_spec((1, D_B)),
            _const_spec((1, D_B)),
            _const_spec((D_B, D_MODEL)),
            _const_spec((D_MODEL, D_MODEL)),
        ],
        out_specs=pl.BlockSpec((1, tm, D_MODEL), lambda b, t: (b, t, 0)),
        out_shape=jax.ShapeDtypeStruct((B, n_rows, D_MODEL), F32),
        compiler_params=_params(("parallel", "parallel")),
        name="outproj",
    )(h, attn, rest, rest, band, lw["wa"], lw["pw"], lw["pb"], lw["ps"], lw["wb"], lw["wo"])
    return out


def _prep_layer_weights(l, tm, norm_gain, w_in, q_norm_gain, k_norm_gain, pool_w, pool_b, pool_scale,
                        w_branch_a, w_branch_b, w_out):
    w = w_in[l]
    o_q, o_ga = 0, D_A + 2 * HEAD_DIM
    o_ub = o_ga + D_A
    o_gb = o_ub + D_B
    o_qi = o_gb + D_B
    o_ki = o_qi + D_A
    o_wi = o_ki + HEAD_DIM
    o_mg = o_wi + N_HEADS
    w_t = jnp.concatenate([w[:, o_q:o_ga], w[:, o_qi:o_mg],
                           jnp.zeros((D_MODEL, N_T_ROWS - (o_ga - o_q) - (o_mg - o_qi)), w.dtype)], axis=1).T
    w_s = jnp.concatenate([w[:, o_ga:o_qi], w[:, o_mg:]], axis=1)
    return {
        "g": norm_gain[l].reshape(1, D_MODEL).astype(F32),
        "wt": w_t.astype(MXU_DTYPE),
        "ws": w_s.astype(MXU_DTYPE),
        "gq": jnp.broadcast_to(q_norm_gain[l].astype(F32)[:, None], (HEAD_DIM, tm)),
        "gk": jnp.broadcast_to(k_norm_gain[l].astype(F32)[:, None], (HEAD_DIM, tm)),
        "wa": w_branch_a[l].astype(MXU_DTYPE),
        "pw": pool_w[l].astype(MXU_DTYPE),
        "pb": pool_b[l].reshape(1, D_B).astype(F32),
        "ps": pool_scale[l].reshape(1, D_B).astype(F32),
        "wb": w_branch_b[l].astype(MXU_DTYPE),
        "wo": w_out[l].astype(MXU_DTYPE),
    }


@jax.jit
def kernel(x, meta_tokens, norm_gain, w_in, q_norm_gain, k_norm_gain, pool_w, pool_b, pool_scale,
           w_branch_a, w_branch_b, w_out):
    B, S, _ = x.shape
    assert S % LANES == 0, "sequence length must be a multiple of 128"
    k_top = min(TOPK_MAX, S // 4)
    n_rows = LANES + S
    tm = _row_tile(n_rows)

    meta = jnp.broadcast_to(meta_tokens.astype(x.dtype)[None], (B, N_META, D_MODEL))
    h = jnp.concatenate([jnp.zeros((B, PAD, D_MODEL), x.dtype), meta, x], axis=1)

    pos = jnp.maximum(jnp.arange(n_rows, dtype=F32) - PAD, 0.0)
    inv_freq = 1.0 / (ROPE_THETA ** (jnp.arange(0, HEAD_DIM, 2, dtype=F32) / HEAD_DIM))
    ang = inv_freq[:, None] * pos[None, :]
    cos_t, sin_t = jnp.cos(ang), jnp.sin(ang)

    r = jnp.arange(tm)[:, None]
    c = jnp.arange(LANES + tm)[None, :] - LANES
    band = jnp.stack([((c <= r) & (c > r - w)) for w in POOL_WINDOWS]).astype(MXU_DTYPE)

    for l in range(DEPTH):
        lw = _prep_layer_weights(l, tm, norm_gain, w_in, q_norm_gain, k_norm_gain, pool_w, pool_b, pool_scale,
                                 w_branch_a, w_branch_b, w_out)
        h = _layer(h, lw, cos_t, sin_t, band, k_top=k_top)
    return h[:, LANES:]
```

```python
import functools
import math

import jax
import jax.numpy as jnp
import numpy as np
from jax import lax
from jax.experimental import pallas as pl
from jax.experimental.pallas import tpu as pltpu

D_MODEL = 1024
DEPTH = 4
CHUNK = 64
N_META = 16
N_HEADS = 8
HEAD_DIM = 64
D_A = N_HEADS * HEAD_DIM
D_B = 512
POOL_WINDOWS = (2, 4, 8, 16)
POOL_GROUP = 128
TOPK_MAX = 256
ROPE_THETA = 10000.0
EPS = 1e-6
NEG_INF = -1e30

LANES = 128
PAD = LANES - N_META
KB = LANES
QB = LANES
V_ROWS = HEAD_DIM + 16
N_T_ROWS = 2 * D_A + 3 * HEAD_DIM + 16
D_REST = D_A + D_B + D_B + 2 * D_MODEL
INT_MIN = -2 ** 31
INT_MAX = 2 ** 31 - 1
FLOAT_MIDPOINT_STEPS = 24
BISECT_STEPS_PER_TEST = 4
MAX_BISECT_STEPS = FLOAT_MIDPOINT_STEPS + 36
VMEM_LIMIT = 48 * 1024 * 1024

MXU_DTYPE = jnp.bfloat16

F32 = jnp.float32
I32 = jnp.int32


def _dot(a, b):
    return jnp.dot(a, b, preferred_element_type=F32)


def _sortable(bits):
    return bits ^ ((bits >> 31) & 0x7FFFFFFF)


_NEG_KEY = int(_sortable(np.float32(NEG_INF).view(np.int32)))


def _proj_kernel(x_ref, g_ref, wt_ref, ws_ref, gq_ref, gk_ref, cos_ref, sin_ref,
                 qt_ref, qit_ref, wit_ref, kk_ref, vt_ref, rest_ref, *, tm):
    x = x_ref[0]
    ms = jnp.mean(x * x, axis=-1, keepdims=True)
    hn = (x * lax.rsqrt(ms + EPS) * g_ref[...]).astype(MXU_DTYPE)

    for c in range(0, D_REST, 512):
        rest_ref[0, :, c:c + 512] = _dot(hn, ws_ref[:, c:c + 512]).astype(rest_ref.dtype)

    pt = lax.dot_general(wt_ref[...], hn, (((1,), (1,)), ((), ())), preferred_element_type=F32)
    cos = cos_ref[...]
    sin = sin_ref[...]
    half = HEAD_DIM // 2

    def rope(t):
        x1, x2 = t[:half], t[half:]
        return jnp.concatenate([x1 * cos - x2 * sin, x2 * cos + x1 * sin], axis=0)

    def norm(t, g):
        return t * lax.rsqrt(jnp.mean(t * t, axis=0, keepdims=True) + EPS) * g

    gq = gq_ref[...]
    q_scale = (HEAD_DIM ** -0.5) * math.log2(math.e)
    for h in range(N_HEADS):
        r0 = h * HEAD_DIM
        qh = rope(norm(pt[r0:r0 + HEAD_DIM], gq)) * q_scale
        qt_ref[0, r0:r0 + HEAD_DIM, :] = qh.astype(qt_ref.dtype)
        qih = rope(pt[D_A + 2 * HEAD_DIM + r0:D_A + 2 * HEAD_DIM + r0 + HEAD_DIM])
        qit_ref[0, r0:r0 + HEAD_DIM, :] = qih.astype(qit_ref.dtype)

    k_t = rope(norm(pt[D_A:D_A + HEAD_DIM], gk_ref[...]))
    v_t = pt[D_A + HEAD_DIM:D_A + 2 * HEAD_DIM]
    r_ki = 2 * D_A + 2 * HEAD_DIM
    ki_t = rope(pt[r_ki:r_ki + HEAD_DIM])
    wi_t = pt[r_ki + HEAD_DIM:r_ki + HEAD_DIM + N_HEADS]
    wit_ref[0] = wi_t * ((N_HEADS ** -0.5) * (HEAD_DIM ** -0.5))

    kk_t = jnp.concatenate([k_t, ki_t], axis=0)
    for c in range(tm // LANES):
        sl = slice(c * LANES, (c + 1) * LANES)
        kk_ref[0, sl, :] = kk_t[:, sl].T.astype(kk_ref.dtype)
        vt_ref[0, c] = v_t[:, sl].astype(vt_ref.dtype)


def _fold8(x, op):
    parts = [x[r:r + 8] for r in range(0, x.shape[0], 8)]
    while len(parts) > 1:
        nxt = [op(parts[a], parts[a + 1]) for a in range(0, len(parts) - 1, 2)]
        if len(parts) % 2:
            nxt.append(parts[-1])
        parts = nxt
    return parts[0]


def _loop_by_two(n, body, init):
    carry = lax.fori_loop(0, n // 2, lambda t, c: body(2 * t + 1, body(2 * t, c)), init)
    return lax.cond(n % 2 == 1, lambda c: body(n - 1, c), lambda c: c, carry)


def _attn_kernel(qt_ref, qit_ref, wit_ref, kk_ref, vt_ref, o_ref, keys_scr, s_scr, acc_scr, *,
                 k_top, n_rows, ks):
    i = pl.program_id(1)
    nkb = ks // KB
    ns = i // nkb + 1
    n_keys = n_rows - PAD
    n_unseen = n_rows - ns * ks
    lane = lax.broadcasted_iota(I32, (1, QB), 1)
    sub = lax.broadcasted_iota(I32, (KB, QB), 0)
    q_row = i * QB + lane
    vis_end = jnp.where(q_row < LANES, LANES, ((q_row - LANES) // CHUNK + 1) * CHUNK + LANES)
    n_vis = vis_end - PAD

    zeros_half = jnp.zeros((HEAD_DIM, N_HEADS * QB), MXU_DTYPE)

    def stack_heads(ref):
        return jnp.concatenate([ref[0, h * HEAD_DIM:(h + 1) * HEAD_DIM, :] for h in range(N_HEADS)], axis=1)

    q_all = jnp.concatenate([stack_heads(qt_ref), zeros_half], axis=0)
    qi_all = jnp.concatenate([zeros_half, stack_heads(qit_ref)], axis=0)
    w = wit_ref[0]

    def score_body(j, carry):
        kmin, kmax = carry
        base = pl.multiple_of(j * ks, ks)
        kk_sb = kk_ref[0, pl.ds(base, ks), :]
        s = _dot(kk_sb, qi_all)
        score = jnp.maximum(s[:, :QB], 0.0) * w[0:1, :]
        for h in range(1, N_HEADS):
            score = score + jnp.maximum(s[:, h * QB:(h + 1) * QB], 0.0) * w[h:h + 1, :]
        for r in range(nkb):
            off = base + r * KB
            vis = sub < (vis_end - off)
            sc = jnp.where(vis, score[r * KB:(r + 1) * KB] + 0.0, NEG_INF)
            key = _sortable(lax.bitcast_convert_type(sc, I32))
            keys_scr[pl.ds(pl.multiple_of(off, KB), KB), :] = key
            kmax = jnp.maximum(kmax, _fold8(key, jnp.maximum))
            kmin = jnp.minimum(kmin, _fold8(jnp.where(vis, key, INT_MAX), jnp.minimum))
        return kmin, kmax

    kmin, kmax = _loop_by_two(ns, score_body,
                              (jnp.full((8, QB), INT_MAX, I32), jnp.full((8, QB), INT_MIN, I32)))
    keys_scr[0:PAD, :] = jnp.full((PAD, QB), INT_MIN, I32)
    kmin = jnp.min(kmin, axis=0, keepdims=True)
    kmax = jnp.max(kmax, axis=0, keepdims=True)

    @pl.when(ns % 2 == 1)
    def _():
        keys_scr[pl.ds(pl.multiple_of(ns * ks, ks), ks), :] = jnp.full((ks, QB), INT_MIN, I32)

    def count(hit):
        def body(t, acc):
            base = pl.multiple_of(t * (2 * ks), 2 * ks)
            for r in range(2 * nkb):
                off = base + r * KB
                kb = keys_scr[pl.ds(pl.multiple_of(off, KB), KB), :]
                acc = acc + _fold8(hit(kb, off), jnp.add)
            return acc
        acc = lax.fori_loop(0, (ns + 1) // 2, body, jnp.zeros((8, QB), I32))
        return jnp.sum(acc, axis=0, keepdims=True)

    def count_ge(cand):
        c = count(lambda kb, off: jnp.where(kb >= cand, 1, 0))
        return c + jnp.where(_NEG_KEY >= cand, n_unseen, 0)

    few_visible = n_vis < k_top
    lo0 = jnp.where(few_visible, jnp.minimum(kmin, _NEG_KEY), kmin)
    c_lo0 = n_vis + jnp.where(_NEG_KEY >= lo0, n_keys - n_vis, 0)
    top = jnp.maximum(kmax, _NEG_KEY)
    hi0 = jnp.where(top == INT_MAX, INT_MAX, top + 1)

    def finished(lo, hi, c_lo):
        return (c_lo == k_top) | (lo + 1 >= hi)

    def bisect_cond(st):
        p, lo, hi, c_lo, _ = st
        pending = jnp.where(finished(lo, hi, c_lo), 0, 1)
        return (p < MAX_BISECT_STEPS) & (jnp.max(pending) > 0)

    def bisect_body(st):
        for _ in range(BISECT_STEPS_PER_TEST):
            st = bisect_step(st)
        return st

    def bisect_step(st):
        p, lo, hi, c_lo, c_hi = st
        done = finished(lo, hi, c_lo)
        lo_f = lax.bitcast_convert_type(_sortable(lo), F32)
        hi_f = lax.bitcast_convert_type(_sortable(hi), F32)
        cand_f = _sortable(lax.bitcast_convert_type(lo_f * 0.5 + hi_f * 0.5, I32))
        cand_k = (lo >> 1) + (hi >> 1) + (lo & hi & 1)
        use_f = (cand_f > lo) & (cand_f < hi) & (p < FLOAT_MIDPOINT_STEPS)
        cand = jnp.where(use_f, cand_f, cand_k)
        cand = jnp.where((p == 0) & (lo < 0) & (hi > 0), 0, cand)
        cand = jnp.where((p == 1) & (lo == 0) & (hi > 1), 1, cand)
        cand = jnp.where(few_visible & (p == 0), _NEG_KEY + 1, cand)
        cand = jnp.minimum(jnp.maximum(cand, lo + 1), hi - 1)
        c = count_ge(cand)
        up = (c >= k_top) & jnp.logical_not(done)
        dn = (c < k_top) & jnp.logical_not(done)
        return (p + 1, jnp.where(up, cand, lo), jnp.where(dn, cand, hi),
                jnp.where(up, c, c_lo), jnp.where(dn, c, c_hi))

    _, tau, _, c_ge, c_gt = lax.while_loop(
        bisect_cond, bisect_body, (jnp.int32(0), lo0, hi0, c_lo0, jnp.zeros((1, QB), I32)))

    n_tied_wanted = jnp.where(c_ge > k_top, k_top - c_gt, n_rows).astype(F32)
    row_ge_col = (lax.broadcasted_iota(I32, (KB, KB), 0) >= lax.broadcasted_iota(I32, (KB, KB), 1))
    prefix_ones = jnp.where(row_ge_col, 1.0, 0.0).astype(MXU_DTYPE)

    def logits_body(j, carry):
        m, tied_before = carry
        base = pl.multiple_of(j * ks, ks)
        bias = []
        for r in range(nkb):
            off = base + r * KB
            kb = keys_scr[pl.ds(pl.multiple_of(off, KB), KB), :]
            tied = jnp.where(kb == tau, 1.0, 0.0).astype(MXU_DTYPE)
            rank = _dot(prefix_ones, tied)
            admissible = rank + tied_before <= n_tied_wanted
            tied_before = tied_before + rank[KB - 1:KB, :]
            thr = tau - jnp.where(admissible, 1, 0)
            thr = jnp.where(sub < vis_end - off, thr, INT_MAX)
            bias.append(jnp.where(kb > thr, 0.0, NEG_INF))
        bias = jnp.concatenate(bias, axis=0)
        s = _dot(kk_ref[0, pl.ds(base, ks), :], q_all) + jnp.concatenate([bias] * N_HEADS, axis=1)
        s_scr[pl.ds(base, ks), :] = s
        return jnp.maximum(m, jnp.max(s, axis=0, keepdims=True)), tied_before

    m, _ = _loop_by_two(ns, logits_body,
                        (jnp.full((1, N_HEADS * QB), NEG_INF, F32), jnp.zeros((1, QB), F32)))

    acc_scr[...] = jnp.zeros_like(acc_scr)
    ones_rows = jnp.ones((V_ROWS - HEAD_DIM, ks), MXU_DTYPE)

    def value_body(j, carry):
        base = pl.multiple_of(j * ks, ks)
        p = jnp.exp2(s_scr[pl.ds(base, ks), :] - m).astype(MXU_DTYPE)
        v_ext = jnp.concatenate(
            [jnp.concatenate([vt_ref[0, j * nkb + r] for r in range(nkb)], axis=1), ones_rows], axis=0)
        acc_scr[...] += _dot(v_ext, p)
        return carry

    _loop_by_two(ns, value_body, 0)

    acc = acc_scr[...]
    out_t = acc[:HEAD_DIM] / acc[HEAD_DIM:HEAD_DIM + 1]
    for hp in range(N_HEADS // 2):
        blk = jnp.concatenate([out_t[:, (2 * hp) * QB:(2 * hp + 1) * QB],
                               out_t[:, (2 * hp + 1) * QB:(2 * hp + 2) * QB]], axis=0)
        o_ref[0, :, hp * LANES:(hp + 1) * LANES] = blk.T.astype(o_ref.dtype)


def _out_kernel(x_ref, attn_ref, rest_ref, halo_ref, band_ref, wa_ref, pw_ref, pb_ref, ps_ref, wb_ref,
                wo_ref, o_ref, *, tm):
    t = pl.program_id(1)
    row = t * tm + lax.broadcasted_iota(I32, (tm, LANES), 0)
    pos = row - PAD

    gate_a = rest_ref[0, :, 0:D_A].astype(F32)
    a = attn_ref[0].astype(F32) * (gate_a * jax.nn.sigmoid(gate_a))
    y_a = _dot(a.astype(MXU_DTYPE), wa_ref[...])

    u = rest_ref[0, :, D_A:D_A + D_B]
    halo = jnp.where(t > 0, halo_ref[0], jnp.zeros_like(halo_ref[0]))
    u_ext = jnp.concatenate([halo, u], axis=0)
    mixed = []
    for g, win in enumerate(POOL_WINDOWS):
        cs = slice(g * POOL_GROUP, (g + 1) * POOL_GROUP)
        window_sum = _dot(band_ref[g], u_ext[:, cs])
        cnt = jnp.clip(pos + 1, 1, win).astype(F32)
        pooled = window_sum / cnt - u[:, cs].astype(F32)
        mixed.append(_dot(pooled.astype(MXU_DTYPE), pw_ref[g]))
    mixed = (jnp.concatenate(mixed, axis=1) + pb_ref[...]) * ps_ref[...]
    gate_b = rest_ref[0, :, D_A + D_B:D_A + 2 * D_B].astype(F32)
    z_b = mixed * (gate_b * jax.nn.sigmoid(gate_b))
    y_b = _dot(z_b.astype(MXU_DTYPE), wb_ref[...])

    c0 = D_A + 2 * D_B
    g_a = jax.nn.sigmoid(rest_ref[0, :, c0:c0 + D_MODEL].astype(F32))
    g_b = jax.nn.sigmoid(rest_ref[0, :, c0 + D_MODEL:c0 + 2 * D_MODEL].astype(F32))
    mix = g_a * y_a + g_b * y_b
    out = x_ref[0] + _dot(mix.astype(MXU_DTYPE), wo_ref[...])
    row_full = t * tm + lax.broadcasted_iota(I32, (tm, D_MODEL), 0)
    o_ref[0] = jnp.where(row_full >= PAD, out, 0.0)


def _row_tile(n_rows):
    for tm in (384, 256, 128):
        if n_rows % tm == 0:
            return tm
    raise ValueError(f"sequence rows {n_rows} must be a multiple of 128")


def _key_step(n_rows):
    return 3 * KB if n_rows % (3 * KB) == 0 else KB


def _params(sem):
    return pltpu.CompilerParams(dimension_semantics=sem, vmem_limit_bytes=VMEM_LIMIT)


def _const_spec(shape):
    return pl.BlockSpec(shape, lambda b, t: (0,) * len(shape))


def _layer(h, lw, cos_t, sin_t, band, *, k_top):
    B, n_rows, _ = h.shape
    tm = _row_tile(n_rows)
    nt = n_rows // tm
    nblk = n_rows // LANES
    act = MXU_DTYPE

    qt, qit, wit, kk, vt, rest = pl.pallas_call(
        functools.partial(_proj_kernel, tm=tm),
        grid=(B, nt),
        in_specs=[
            pl.BlockSpec((1, tm, D_MODEL), lambda b, t: (b, t, 0)),
            _const_spec((1, D_MODEL)),
            _const_spec((N_T_ROWS, D_MODEL)),
            _const_spec((D_MODEL, D_REST)),
            _const_spec((HEAD_DIM, tm)),
            _const_spec((HEAD_DIM, tm)),
            pl.BlockSpec((HEAD_DIM // 2, tm), lambda b, t: (0, t)),
            pl.BlockSpec((HEAD_DIM // 2, tm), lambda b, t: (0, t)),
        ],
        out_specs=[
            pl.BlockSpec((1, D_A, tm), lambda b, t: (b, 0, t)),
            pl.BlockSpec((1, D_A, tm), lambda b, t: (b, 0, t)),
            pl.BlockSpec((1, N_HEADS, tm), lambda b, t: (b, 0, t)),
            pl.BlockSpec((1, tm, LANES), lambda b, t: (b, t, 0)),
            pl.BlockSpec((1, tm // LANES, HEAD_DIM, LANES), lambda b, t: (b, t, 0, 0)),
            pl.BlockSpec((1, tm, D_REST), lambda b, t: (b, t, 0)),
        ],
        out_shape=[
            jax.ShapeDtypeStruct((B, D_A, n_rows), act),
            jax.ShapeDtypeStruct((B, D_A, n_rows), act),
            jax.ShapeDtypeStruct((B, N_HEADS, n_rows), F32),
            jax.ShapeDtypeStruct((B, n_rows, LANES), act),
            jax.ShapeDtypeStruct((B, nblk, HEAD_DIM, LANES), act),
            jax.ShapeDtypeStruct((B, n_rows, D_REST), act),
        ],
        compiler_params=_params(("parallel", "parallel")),
        name="proj",
    )(h, lw["g"], lw["wt"], lw["ws"], lw["gq"], lw["gk"], cos_t, sin_t)

    assert n_rows - PAD >= k_top
    ks = _key_step(n_rows)
    attn = pl.pallas_call(
        functools.partial(_attn_kernel, k_top=k_top, n_rows=n_rows, ks=ks),
        grid=(B, nblk),
        in_specs=[
            pl.BlockSpec((1, D_A, QB), lambda b, i: (b, 0, i)),
            pl.BlockSpec((1, D_A, QB), lambda b, i: (b, 0, i)),
            pl.BlockSpec((1, N_HEADS, QB), lambda b, i: (b, 0, i)),
            pl.BlockSpec((1, n_rows, LANES), lambda b, i: (b, 0, 0)),
            pl.BlockSpec((1, nblk, HEAD_DIM, LANES), lambda b, i: (b, 0, 0, 0)),
        ],
        out_specs=pl.BlockSpec((1, QB, D_A), lambda b, i: (b, i, 0)),
        out_shape=jax.ShapeDtypeStruct((B, n_rows, D_A), act),
        scratch_shapes=[
            pltpu.VMEM((n_rows + ks, QB), I32),
            pltpu.VMEM((n_rows, N_HEADS * QB), F32),
            pltpu.VMEM((V_ROWS, N_HEADS * QB), F32),
        ],
        compiler_params=_params(("parallel", "arbitrary")),
        name="attn",
    )(qt, qit, wit, kk, vt)

    halo_blocks = tm // LANES
    out = pl.pallas_call(
        functools.partial(_out_kernel, tm=tm),
        grid=(B, nt),
        in_specs=[
            pl.BlockSpec((1, tm, D_MODEL), lambda b, t: (b, t, 0)),
            pl.BlockSpec((1, tm, D_A), lambda b, t: (b, t, 0)),
            pl.BlockSpec((1, tm, D_REST), lambda b, t: (b, t, 0)),
            pl.BlockSpec((1, LANES, D_B), lambda b, t: (b, jnp.maximum(t * halo_blocks - 1, 0), 1)),
            _const_spec((len(POOL_WINDOWS), tm, LANES + tm)),
            _const_spec((D_A, D_MODEL)),
            _const_spec((len(POOL_WINDOWS), POOL_GROUP, POOL_GROUP)),
            _const_spec((1, D_B)),
            _const_spec((1, D_B)),
            _const_spec((D_B, D_MODEL)),
            _const_spec((D_MODEL, D_MODEL)),
        ],
        out_specs=pl.BlockSpec((1, tm, D_MODEL), lambda b, t: (b, t, 0)),
        out_shape=jax.ShapeDtypeStruct((B, n_rows, D_MODEL), F32),
        compiler_params=_params(("parallel", "parallel")),
        name="outproj",
    )(h, attn, rest, rest, band, lw["wa"], lw["pw"], lw["pb"], lw["ps"], lw["wb"], lw["wo"])
    return out


def _prep_layer_weights(l, tm, norm_gain, w_in, q_norm_gain, k_norm_gain, pool_w, pool_b, pool_scale,
                        w_branch_a, w_branch_b, w_out):
    w = w_in[l]
    o_q, o_ga = 0, D_A + 2 * HEAD_DIM
    o_ub = o_ga + D_A
    o_gb = o_ub + D_B
    o_qi = o_gb + D_B
    o_ki = o_qi + D_A
    o_wi = o_ki + HEAD_DIM
    o_mg = o_wi + N_HEADS
    w_t = jnp.concatenate([w[:, o_q:o_ga], w[:, o_qi:o_mg],
                           jnp.zeros((D_MODEL, N_T_ROWS - (o_ga - o_q) - (o_mg - o_qi)), w.dtype)], axis=1).T
    w_s = jnp.concatenate([w[:, o_ga:o_qi], w[:, o_mg:]], axis=1)
    return {
        "g": norm_gain[l].reshape(1, D_MODEL).astype(F32),
        "wt": w_t.astype(MXU_DTYPE),
        "ws": w_s.astype(MXU_DTYPE),
        "gq": jnp.broadcast_to(q_norm_gain[l].astype(F32)[:, None], (HEAD_DIM, tm)),
        "gk": jnp.broadcast_to(k_norm_gain[l].astype(F32)[:, None], (HEAD_DIM, tm)),
        "wa": w_branch_a[l].astype(MXU_DTYPE),
        "pw": pool_w[l].astype(MXU_DTYPE),
        "pb": pool_b[l].reshape(1, D_B).astype(F32),
        "ps": pool_scale[l].reshape(1, D_B).astype(F32),
        "wb": w_branch_b[l].astype(MXU_DTYPE),
        "wo": w_out[l].astype(MXU_DTYPE),
    }


@jax.jit
def kernel(x, meta_tokens, norm_gain, w_in, q_norm_gain, k_norm_gain, pool_w, pool_b, pool_scale,
           w_branch_a, w_branch_b, w_out):
    B, S, _ = x.shape
    assert S % LANES == 0, "sequence length must be a multiple of 128"
    k_top = min(TOPK_MAX, S // 4)
    n_rows = LANES + S
    tm = _row_tile(n_rows)

    meta = jnp.broadcast_to(meta_tokens.astype(x.dtype)[None], (B, N_META, D_MODEL))
    h = jnp.concatenate([jnp.zeros((B, PAD, D_MODEL), x.dtype), meta, x], axis=1)

    pos = jnp.maximum(jnp.arange(n_rows, dtype=F32) - PAD, 0.0)
    inv_freq = 1.0 / (ROPE_THETA ** (jnp.arange(0, HEAD_DIM, 2, dtype=F32) / HEAD_DIM))
    ang = inv_freq[:, None] * pos[None, :]
    cos_t, sin_t = jnp.cos(ang), jnp.sin(ang)

    r = jnp.arange(tm)[:, None]
    c = jnp.arange(LANES + tm)[None, :] - LANES
    band = jnp.stack([((c <= r) & (c > r - w)) for w in POOL_WINDOWS]).astype(MXU_DTYPE)

    for l in range(DEPTH):
        lw = _prep_layer_weights(l, tm, norm_gain, w_in, q_norm_gain, k_norm_gain, pool_w, pool_b, pool_scale,
                                 w_branch_a, w_branch_b, w_out)
        h = _layer(h, lw, cos_t, sin_t, band, k_top=k_top)
    return h[:, LANES:]
```

```python
import functools
import math

import jax
import jax.numpy as jnp
import numpy as np
from jax import lax
from jax.experimental import pallas as pl
from jax.experimental.pallas import tpu as pltpu

D_MODEL = 1024
DEPTH = 4
CHUNK = 64
N_META = 16
N_HEADS = 8
HEAD_DIM = 64
D_A = N_HEADS * HEAD_DIM
D_B = 512
POOL_WINDOWS = (2, 4, 8, 16)
POOL_GROUP = 128
TOPK_MAX = 256
ROPE_THETA = 10000.0
EPS = 1e-6
NEG_INF = -1e30

LANES = 128
PAD = LANES - N_META
KB = LANES
QB = LANES
V_ROWS = HEAD_DIM + 16
N_T_ROWS = 2 * D_A + 3 * HEAD_DIM + 16
D_REST = D_A + D_B + D_B + 2 * D_MODEL
INT_MIN = -2 ** 31
INT_MAX = 2 ** 31 - 1
FLOAT_MIDPOINT_STEPS = 24
BISECT_STEPS_PER_TEST = 4
MAX_BISECT_STEPS = FLOAT_MIDPOINT_STEPS + 36
VMEM_LIMIT = 48 * 1024 * 1024

MXU_DTYPE = jnp.bfloat16

F32 = jnp.float32
I32 = jnp.int32


def _dot(a, b):
    return jnp.dot(a, b, preferred_element_type=F32)


def _sortable(bits):
    return bits ^ ((bits >> 31) & 0x7FFFFFFF)


_NEG_KEY = int(_sortable(np.float32(NEG_INF).view(np.int32)))


def _proj_kernel(x_ref, g_ref, wt_ref, ws_ref, gq_ref, gk_ref, cos_ref, sin_ref,
                 qt_ref, qit_ref, wit_ref, kk_ref, vt_ref, rest_ref, *, tm):
    x = x_ref[0]
    ms = jnp.mean(x * x, axis=-1, keepdims=True)
    hn = (x * lax.rsqrt(ms + EPS) * g_ref[...]).astype(MXU_DTYPE)

    for c in range(0, D_REST, 512):
        rest_ref[0, :, c:c + 512] = _dot(hn, ws_ref[:, c:c + 512]).astype(rest_ref.dtype)

    pt = lax.dot_general(wt_ref[...], hn, (((1,), (1,)), ((), ())), preferred_element_type=F32)
    cos = cos_ref[...]
    sin = sin_ref[...]
    half = HEAD_DIM // 2

    def rope(t):
        x1, x2 = t[:half], t[half:]
        return jnp.concatenate([x1 * cos - x2 * sin, x2 * cos + x1 * sin], axis=0)

    def norm(t, g):
        return t * lax.rsqrt(jnp.mean(t * t, axis=0, keepdims=True) + EPS) * g

    gq = gq_ref[...]
    q_scale = (HEAD_DIM ** -0.5) * math.log2(math.e)
    for h in range(N_HEADS):
        r0 = h * HEAD_DIM
        qh = rope(norm(pt[r0:r0 + HEAD_DIM], gq)) * q_scale
        qt_ref[0, r0:r0 + HEAD_DIM, :] = qh.astype(qt_ref.dtype)
        qih = rope(pt[D_A + 2 * HEAD_DIM + r0:D_A + 2 * HEAD_DIM + r0 + HEAD_DIM])
        qit_ref[0, r0:r0 + HEAD_DIM, :] = qih.astype(qit_ref.dtype)

    k_t = rope(norm(pt[D_A:D_A + HEAD_DIM], gk_ref[...]))
    v_t = pt[D_A + HEAD_DIM:D_A + 2 * HEAD_DIM]
    r_ki = 2 * D_A + 2 * HEAD_DIM
    ki_t = rope(pt[r_ki:r_ki + HEAD_DIM])
    wi_t = pt[r_ki + HEAD_DIM:r_ki + HEAD_DIM + N_HEADS]
    wit_ref[0] = wi_t * ((N_HEADS ** -0.5) * (HEAD_DIM ** -0.5))

    kk_t = jnp.concatenate([k_t, ki_t], axis=0)
    for c in range(tm // LANES):
        sl = slice(c * LANES, (c + 1) * LANES)
        kk_ref[0, sl, :] = kk_t[:, sl].T.astype(kk_ref.dtype)
        vt_ref[0, c] = v_t[:, sl].astype(vt_ref.dtype)


def _fold8(x, op):
    parts = [x[r:r + 8] for r in range(0, x.shape[0], 8)]
    while len(parts) > 1:
        nxt = [op(parts[a], parts[a + 1]) for a in range(0, len(parts) - 1, 2)]
        if len(parts) % 2:
            nxt.append(parts[-1])
        parts = nxt
    return parts[0]


def _loop_by_two(n, body, init):
    carry = lax.fori_loop(0, n // 2, lambda t, c: body(2 * t + 1, body(2 * t, c)), init)
    return lax.cond(n % 2 == 1, lambda c: body(n - 1, c), lambda c: c, carry)


def _attn_kernel(qt_ref, qit_ref, wit_ref, kk_ref, vt_ref, o_ref, keys_scr, s_scr, acc_scr, *,
                 k_top, n_rows, ks):
    i = pl.program_id(1)
    nkb = ks // KB
    ns = i // nkb + 1
    n_keys = n_rows - PAD
    n_unseen = n_rows - ns * ks
    lane = lax.broadcasted_iota(I32, (1, QB), 1)
    sub = lax.broadcasted_iota(I32, (KB, QB), 0)
    q_row = i * QB + lane
    vis_end = jnp.where(q_row < LANES, LANES, ((q_row - LANES) // CHUNK + 1) * CHUNK + LANES)
    n_vis = vis_end - PAD

    zeros_half = jnp.zeros((HEAD_DIM, N_HEADS * QB), MXU_DTYPE)

    def stack_heads(ref):
        return jnp.concatenate([ref[0, h * HEAD_DIM:(h + 1) * HEAD_DIM, :] for h in range(N_HEADS)], axis=1)

    q_all = jnp.concatenate([stack_heads(qt_ref), zeros_half], axis=0)
    qi_all = jnp.concatenate([zeros_half, stack_heads(qit_ref)], axis=0)
    w = wit_ref[0]

    def score_body(j, carry):
        kmin, kmax = carry
        base = pl.multiple_of(j * ks, ks)
        kk_sb = kk_ref[0, pl.ds(base, ks), :]
        s = _dot(kk_sb, qi_all)
        score = jnp.maximum(s[:, :QB], 0.0) * w[0:1, :]
        for h in range(1, N_HEADS):
            score = score + jnp.maximum(s[:, h * QB:(h + 1) * QB], 0.0) * w[h:h + 1, :]
        for r in range(nkb):
            off = base + r * KB
            vis = sub < (vis_end - off)
            sc = jnp.where(vis, score[r * KB:(r + 1) * KB] + 0.0, NEG_INF)
            key = _sortable(lax.bitcast_convert_type(sc, I32))
            keys_scr[pl.ds(pl.multiple_of(off, KB), KB), :] = key
            kmax = jnp.maximum(kmax, _fold8(key, jnp.maximum))
            kmin = jnp.minimum(kmin, _fold8(jnp.where(vis, key, INT_MAX), jnp.minimum))
        return kmin, kmax

    kmin, kmax = _loop_by_two(ns, score_body,
                              (jnp.full((8, QB), INT_MAX, I32), jnp.full((8, QB), INT_MIN, I32)))
    keys_scr[0:PAD, :] = jnp.full((PAD, QB), INT_MIN, I32)
    kmin = jnp.min(kmin, axis=0, keepdims=True)
    kmax = jnp.max(kmax, axis=0, keepdims=True)

    @pl.when(ns % 2 == 1)
    def _():
        keys_scr[pl.ds(pl.multiple_of(ns * ks, ks), ks), :] = jnp.full((ks, QB), INT_MIN, I32)

    def reduce_keys(term, op, init):
        def body(t, acc):
            base = pl.multiple_of(t * (2 * ks), 2 * ks)
            for r in range(2 * nkb):
                kb = keys_scr[pl.ds(pl.multiple_of(base + r * KB, KB), KB), :]
                acc = op(acc, _fold8(term(kb), op))
            return acc
        return lax.fori_loop(0, (ns + 1) // 2, body, jnp.full((8, QB), init, I32))

    def count_ge(cand):
        c = reduce_keys(lambda kb: jnp.where(kb >= cand, 1, 0), jnp.add, 0)
        return jnp.sum(c, axis=0, keepdims=True) + jnp.where(_NEG_KEY >= cand, n_unseen, 0)

    few_visible = n_vis < k_top
    lo0 = jnp.where(few_visible, jnp.minimum(kmin, _NEG_KEY), kmin)
    c_lo0 = n_vis + jnp.where(_NEG_KEY >= lo0, n_keys - n_vis, 0)
    top = jnp.maximum(kmax, _NEG_KEY)
    hi0 = jnp.where(top == INT_MAX, INT_MAX, top + 1)

    def finished(lo, hi, c_lo, c_hi):
        return (c_lo == k_top) | (lo + 1 >= hi) | (c_hi == k_top - 1)

    def bisect_cond(st):
        p, lo, hi, c_lo, c_hi = st
        pending = jnp.where(finished(lo, hi, c_lo, c_hi), 0, 1)
        return (p < MAX_BISECT_STEPS) & (jnp.max(pending) > 0)

    def bisect_body(st):
        for _ in range(BISECT_STEPS_PER_TEST):
            st = bisect_step(st)
        return st

    def bisect_step(st):
        p, lo, hi, c_lo, c_hi = st
        done = finished(lo, hi, c_lo, c_hi)
        lo_f = lax.bitcast_convert_type(_sortable(lo), F32)
        hi_f = lax.bitcast_convert_type(_sortable(hi), F32)
        cand_f = _sortable(lax.bitcast_convert_type(lo_f * 0.5 + hi_f * 0.5, I32))
        cand_k = (lo >> 1) + (hi >> 1) + (lo & hi & 1)
        use_f = (cand_f > lo) & (cand_f < hi) & (p < FLOAT_MIDPOINT_STEPS)
        cand = jnp.where(use_f, cand_f, cand_k)
        cand = jnp.where((p == 0) & (lo < 0) & (hi > 0), 0, cand)
        cand = jnp.where((p == 1) & (lo == 0) & (hi > 1), 1, cand)
        cand = jnp.where(few_visible & (p == 0), _NEG_KEY + 1, cand)
        cand = jnp.minimum(jnp.maximum(cand, lo + 1), hi - 1)
        c = count_ge(cand)
        up = (c >= k_top) & jnp.logical_not(done)
        dn = (c < k_top) & jnp.logical_not(done)
        return (p + 1, jnp.where(up, cand, lo), jnp.where(dn, cand, hi),
                jnp.where(up, c, c_lo), jnp.where(dn, c, c_hi))

    _, lo, hi, c_lo, c_hi = lax.while_loop(
        bisect_cond, bisect_body, (jnp.int32(0), lo0, hi0, c_lo0, jnp.zeros((1, QB), I32)))

    below_hi = reduce_keys(lambda kb: jnp.where(kb < hi, kb, INT_MIN), jnp.maximum, INT_MIN)
    below_hi = jnp.max(below_hi, axis=0, keepdims=True)
    below_hi = jnp.where((n_unseen > 0) & (_NEG_KEY < hi), jnp.maximum(below_hi, _NEG_KEY), below_hi)
    exact_cut = c_lo == k_top
    tau = jnp.where(exact_cut | (lo + 1 >= hi), lo, below_hi)

    n_tied_wanted = jnp.where(exact_cut, n_rows, k_top - c_hi).astype(F32)
    row_ge_col = (lax.broadcasted_iota(I32, (KB, KB), 0) >= lax.broadcasted_iota(I32, (KB, KB), 1))
    prefix_ones = jnp.where(row_ge_col, 1.0, 0.0).astype(MXU_DTYPE)

    def logits_body(j, carry):
        m, tied_before = carry
        base = pl.multiple_of(j * ks, ks)
        bias = []
        for r in range(nkb):
            off = base + r * KB
            kb = keys_scr[pl.ds(pl.multiple_of(off, KB), KB), :]
            tied = jnp.where(kb == tau, 1.0, 0.0).astype(MXU_DTYPE)
            rank = _dot(prefix_ones, tied)
            admissible = rank + tied_before <= n_tied_wanted
            tied_before = tied_before + rank[KB - 1:KB, :]
            thr = tau - jnp.where(admissible, 1, 0)
            thr = jnp.where(sub < vis_end - off, thr, INT_MAX)
            bias.append(jnp.where(kb > thr, 0.0, NEG_INF))
        bias = jnp.concatenate(bias, axis=0)
        s = _dot(kk_ref[0, pl.ds(base, ks), :], q_all) + jnp.concatenate([bias] * N_HEADS, axis=1)
        s_scr[pl.ds(base, ks), :] = s
        return jnp.maximum(m, jnp.max(s, axis=0, keepdims=True)), tied_before

    m, _ = _loop_by_two(ns, logits_body,
                        (jnp.full((1, N_HEADS * QB), NEG_INF, F32), jnp.zeros((1, QB), F32)))

    acc_scr[...] = jnp.zeros_like(acc_scr)
    ones_rows = jnp.ones((V_ROWS - HEAD_DIM, ks), MXU_DTYPE)

    def value_body(j, carry):
        base = pl.multiple_of(j * ks, ks)
        p = jnp.exp2(s_scr[pl.ds(base, ks), :] - m).astype(MXU_DTYPE)
        v_ext = jnp.concatenate(
            [jnp.concatenate([vt_ref[0, j * nkb + r] for r in range(nkb)], axis=1), ones_rows], axis=0)
        acc_scr[...] += _dot(v_ext, p)
        return carry

    _loop_by_two(ns, value_body, 0)

    acc = acc_scr[...]
    out_t = acc[:HEAD_DIM] / acc[HEAD_DIM:HEAD_DIM + 1]
    for hp in range(N_HEADS // 2):
        blk = jnp.concatenate([out_t[:, (2 * hp) * QB:(2 * hp + 1) * QB],
                               out_t[:, (2 * hp + 1) * QB:(2 * hp + 2) * QB]], axis=0)
        o_ref[0, :, hp * LANES:(hp + 1) * LANES] = blk.T.astype(o_ref.dtype)


def _out_kernel(x_ref, attn_ref, rest_ref, halo_ref, band_ref, wa_ref, pw_ref, pb_ref, ps_ref, wb_ref,
                wo_ref, o_ref, *, tm):
    t = pl.program_id(1)
    row = t * tm + lax.broadcasted_iota(I32, (tm, LANES), 0)
    pos = row - PAD

    gate_a = rest_ref[0, :, 0:D_A].astype(F32)
    a = attn_ref[0].astype(F32) * (gate_a * jax.nn.sigmoid(gate_a))
    y_a = _dot(a.astype(MXU_DTYPE), wa_ref[...])

    u = rest_ref[0, :, D_A:D_A + D_B]
    halo = jnp.where(t > 0, halo_ref[0], jnp.zeros_like(halo_ref[0]))
    u_ext = jnp.concatenate([halo, u], axis=0)
    mixed = []
    for g, win in enumerate(POOL_WINDOWS):
        cs = slice(g * POOL_GROUP, (g + 1) * POOL_GROUP)
        window_sum = _dot(band_ref[g], u_ext[:, cs])
        cnt = jnp.clip(pos + 1, 1, win).astype(F32)
        pooled = window_sum / cnt - u[:, cs].astype(F32)
        mixed.append(_dot(pooled.astype(MXU_DTYPE), pw_ref[g]))
    mixed = (jnp.concatenate(mixed, axis=1) + pb_ref[...]) * ps_ref[...]
    gate_b = rest_ref[0, :, D_A + D_B:D_A + 2 * D_B].astype(F32)
    z_b = mixed * (gate_b * jax.nn.sigmoid(gate_b))
    y_b = _dot(z_b.astype(MXU_DTYPE), wb_ref[...])

    c0 = D_A + 2 * D_B
    g_a = jax.nn.sigmoid(rest_ref[0, :, c0:c0 + D_MODEL].astype(F32))
    g_b = jax.nn.sigmoid(rest_ref[0, :, c0 + D_MODEL:c0 + 2 * D_MODEL].astype(F32))
    mix = g_a * y_a + g_b * y_b
    out = x_ref[0] + _dot(mix.astype(MXU_DTYPE), wo_ref[...])
    row_full = t * tm + lax.broadcasted_iota(I32, (tm, D_MODEL), 0)
    o_ref[0] = jnp.where(row_full >= PAD, out, 0.0)


def _row_tile(n_rows):
    for tm in (384, 256, 128):
        if n_rows % tm == 0:
            return tm
    raise ValueError(f"sequence rows {n_rows} must be a multiple of 128")


def _key_step(n_rows):
    return 3 * KB if n_rows % (3 * KB) == 0 else KB


def _params(sem):
    return pltpu.CompilerParams(dimension_semantics=sem, vmem_limit_bytes=VMEM_LIMIT)


def _const_spec(shape):
    return pl.BlockSpec(shape, lambda b, t: (0,) * len(shape))


def _layer(h, lw, cos_t, sin_t, band, *, k_top):
    B, n_rows, _ = h.shape
    tm = _row_tile(n_rows)
    nt = n_rows // tm
    nblk = n_rows // LANES
    act = MXU_DTYPE

    qt, qit, wit, kk, vt, rest = pl.pallas_call(
        functools.partial(_proj_kernel, tm=tm),
        grid=(B, nt),
        in_specs=[
            pl.BlockSpec((1, tm, D_MODEL), lambda b, t: (b, t, 0)),
            _const_spec((1, D_MODEL)),
            _const_spec((N_T_ROWS, D_MODEL)),
            _const_spec((D_MODEL, D_REST)),
            _const_spec((HEAD_DIM, tm)),
            _const_spec((HEAD_DIM, tm)),
            pl.BlockSpec((HEAD_DIM // 2, tm), lambda b, t: (0, t)),
            pl.BlockSpec((HEAD_DIM // 2, tm), lambda b, t: (0, t)),
        ],
        out_specs=[
            pl.BlockSpec((1, D_A, tm), lambda b, t: (b, 0, t)),
            pl.BlockSpec((1, D_A, tm), lambda b, t: (b, 0, t)),
            pl.BlockSpec((1, N_HEADS, tm), lambda b, t: (b, 0, t)),
            pl.BlockSpec((1, tm, LANES), lambda b, t: (b, t, 0)),
            pl.BlockSpec((1, tm // LANES, HEAD_DIM, LANES), lambda b, t: (b, t, 0, 0)),
            pl.BlockSpec((1, tm, D_REST), lambda b, t: (b, t, 0)),
        ],
        out_shape=[
            jax.ShapeDtypeStruct((B, D_A, n_rows), act),
            jax.ShapeDtypeStruct((B, D_A, n_rows), act),
            jax.ShapeDtypeStruct((B, N_HEADS, n_rows), F32),
            jax.ShapeDtypeStruct((B, n_rows, LANES), act),
            jax.ShapeDtypeStruct((B, nblk, HEAD_DIM, LANES), act),
            jax.ShapeDtypeStruct((B, n_rows, D_REST), act),
        ],
        compiler_params=_params(("parallel", "parallel")),
        name="proj",
    )(h, lw["g"], lw["wt"], lw["ws"], lw["gq"], lw["gk"], cos_t, sin_t)

    assert n_rows - PAD >= k_top
    ks = _key_step(n_rows)
    attn = pl.pallas_call(
        functools.partial(_attn_kernel, k_top=k_top, n_rows=n_rows, ks=ks),
        grid=(B, nblk),
        in_specs=[
            pl.BlockSpec((1, D_A, QB), lambda b, i: (b, 0, i)),
            pl.BlockSpec((1, D_A, QB), lambda b, i: (b, 0, i)),
            pl.BlockSpec((1, N_HEADS, QB), lambda b, i: (b, 0, i)),
            pl.BlockSpec((1, n_rows, LANES), lambda b, i: (b, 0, 0)),
            pl.BlockSpec((1, nblk, HEAD_DIM, LANES), lambda b, i: (b, 0, 0, 0)),
        ],
        out_specs=pl.BlockSpec((1, QB, D_A), lambda b, i: (b, i, 0)),
        out_shape=jax.ShapeDtypeStruct((B, n_rows, D_A), act),
        scratch_shapes=[
            pltpu.VMEM((n_rows + ks, QB), I32),
            pltpu.VMEM((n_rows, N_HEADS * QB), F32),
            pltpu.VMEM((V_ROWS, N_HEADS * QB), F32),
        ],
        compiler_params=_params(("parallel", "arbitrary")),
        name="attn",
    )(qt, qit, wit, kk, vt)

    halo_blocks = tm // LANES
    out = pl.pallas_call(
        functools.partial(_out_kernel, tm=tm),
        grid=(B, nt),
        in_specs=[
            pl.BlockSpec((1, tm, D_MODEL), lambda b, t: (b, t, 0)),
            pl.BlockSpec((1, tm, D_A), lambda b, t: (b, t, 0)),
            pl.BlockSpec((1, tm, D_REST), lambda b, t: (b, t, 0)),
            pl.BlockSpec((1, LANES, D_B), lambda b, t: (b, jnp.maximum(t * halo_blocks - 1, 0), 1)),
            _const_spec((len(POOL_WINDOWS), tm, LANES + tm)),
            _const_spec((D_A, D_MODEL)),
            _const_spec((len(POOL_WINDOWS), POOL_GROUP, POOL_GROUP)),
            _const_spec((1, D_B)),
            _const_spec((1, D_B)),
            _const_spec((D_B, D_MODEL)),
            _const_spec((D_MODEL, D_MODEL)),
        ],
        out_specs=pl.BlockSpec((1, tm, D_MODEL), lambda b, t: (b, t, 0)),
        out_shape=jax.ShapeDtypeStruct((B, n_rows, D_MODEL), F32),
        compiler_params=_params(("parallel", "parallel")),
        name="outproj",
    )(h, attn, rest, rest, band, lw["wa"], lw["pw"], lw["pb"], lw["ps"], lw["wb"], lw["wo"])
    return out


def _prep_layer_weights(l, tm, norm_gain, w_in, q_norm_gain, k_norm_gain, pool_w, pool_b, pool_scale,
                        w_branch_a, w_branch_b, w_out):
    w = w_in[l]
    o_q, o_ga = 0, D_A + 2 * HEAD_DIM
    o_ub = o_ga + D_A
    o_gb = o_ub + D_B
    o_qi = o_gb + D_B
    o_ki = o_qi + D_A
    o_wi = o_ki + HEAD_DIM
    o_mg = o_wi + N_HEADS
    w_t = jnp.concatenate([w[:, o_q:o_ga], w[:, o_qi:o_mg],
                           jnp.zeros((D_MODEL, N_T_ROWS - (o_ga - o_q) - (o_mg - o_qi)), w.dtype)], axis=1).T
    w_s = jnp.concatenate([w[:, o_ga:o_qi], w[:, o_mg:]], axis=1)
    return {
        "g": norm_gain[l].reshape(1, D_MODEL).astype(F32),
        "wt": w_t.astype(MXU_DTYPE),
        "ws": w_s.astype(MXU_DTYPE),
        "gq": jnp.broadcast_to(q_norm_gain[l].astype(F32)[:, None], (HEAD_DIM, tm)),
        "gk": jnp.broadcast_to(k_norm_gain[l].astype(F32)[:, None], (HEAD_DIM, tm)),
        "wa": w_branch_a[l].astype(MXU_DTYPE),
        "pw": pool_w[l].astype(MXU_DTYPE),
        "pb": pool_b[l].reshape(1, D_B).astype(F32),
        "ps": pool_scale[l].reshape(1, D_B).astype(F32),
        "wb": w_branch_b[l].astype(MXU_DTYPE),
        "wo": w_out[l].astype(MXU_DTYPE),
    }


@jax.jit
def kernel(x, meta_tokens, norm_gain, w_in, q_norm_gain, k_norm_gain, pool_w, pool_b, pool_scale,
           w_branch_a, w_branch_b, w_out):
    B, S, _ = x.shape
    assert S % LANES == 0, "sequence length must be a multiple of 128"
    k_top = min(TOPK_MAX, S // 4)
    n_rows = LANES + S
    tm = _row_tile(n_rows)

    meta = jnp.broadcast_to(meta_tokens.astype(x.dtype)[None], (B, N_META, D_MODEL))
    h = jnp.concatenate([jnp.zeros((B, PAD, D_MODEL), x.dtype), meta, x], axis=1)

    pos = jnp.maximum(jnp.arange(n_rows, dtype=F32) - PAD, 0.0)
    inv_freq = 1.0 / (ROPE_THETA ** (jnp.arange(0, HEAD_DIM, 2, dtype=F32) / HEAD_DIM))
    ang = inv_freq[:, None] * pos[None, :]
    cos_t, sin_t = jnp.cos(ang), jnp.sin(ang)

    r = jnp.arange(tm)[:, None]
    c = jnp.arange(LANES + tm)[None, :] - LANES
    band = jnp.stack([((c <= r) & (c > r - w)) for w in POOL_WINDOWS]).astype(MXU_DTYPE)

    for l in range(DEPTH):
        lw = _prep_layer_weights(l, tm, norm_gain, w_in, q_norm_gain, k_norm_gain, pool_w, pool_b, pool_scale,
                                 w_branch_a, w_branch_b, w_out)
        h = _layer(h, lw, cos_t, sin_t, band, k_top=k_top)
    return h[:, LANES:]
```

```python
import functools
import math

import jax
import jax.numpy as jnp
import numpy as np
from jax import lax
from jax.experimental import pallas as pl
from jax.experimental.pallas import tpu as pltpu

D_MODEL = 1024
DEPTH = 4
CHUNK = 64
N_META = 16
N_HEADS = 8
HEAD_DIM = 64
D_A = N_HEADS * HEAD_DIM
D_B = 512
POOL_WINDOWS = (2, 4, 8, 16)
POOL_GROUP = 128
TOPK_MAX = 256
ROPE_THETA = 10000.0
EPS = 1e-6
NEG_INF = -1e30

LANES = 128
PAD = LANES - N_META
KB = LANES
QB = LANES
V_ROWS = HEAD_DIM + 16
N_T_ROWS = 2 * D_A + 3 * HEAD_DIM + 16
D_REST = D_A + D_B + D_B + 2 * D_MODEL
INT_MIN = -2 ** 31
INT_MAX = 2 ** 31 - 1
FLOAT_MIDPOINT_STEPS = 24
BISECT_STEPS_PER_TEST = 4
UNTESTED_BISECT_STEPS = 12
ZERO_TIE_MARGIN = 128
MAX_BISECT_STEPS = FLOAT_MIDPOINT_STEPS + 36
VMEM_LIMIT = 48 * 1024 * 1024

MXU_DTYPE = jnp.bfloat16

F32 = jnp.float32
I32 = jnp.int32


def _dot(a, b):
    return jnp.dot(a, b, preferred_element_type=F32)


def _sortable(bits):
    return bits ^ ((bits >> 31) & 0x7FFFFFFF)


_NEG_KEY = int(_sortable(np.float32(NEG_INF).view(np.int32)))


def _proj_kernel(x_ref, g_ref, wt_ref, ws_ref, gq_ref, gk_ref, cos_ref, sin_ref,
                 qt_ref, qit_ref, wit_ref, kk_ref, vt_ref, rest_ref, *, tm):
    x = x_ref[0]
    ms = jnp.mean(x * x, axis=-1, keepdims=True)
    hn = (x * lax.rsqrt(ms + EPS) * g_ref[...]).astype(MXU_DTYPE)

    for c in range(0, D_REST, 512):
        rest_ref[0, :, c:c + 512] = _dot(hn, ws_ref[:, c:c + 512]).astype(rest_ref.dtype)

    pt = lax.dot_general(wt_ref[...], hn, (((1,), (1,)), ((), ())), preferred_element_type=F32)
    cos = cos_ref[...]
    sin = sin_ref[...]
    half = HEAD_DIM // 2

    def rope(t):
        x1, x2 = t[:half], t[half:]
        return jnp.concatenate([x1 * cos - x2 * sin, x2 * cos + x1 * sin], axis=0)

    def norm(t, g):
        return t * lax.rsqrt(jnp.mean(t * t, axis=0, keepdims=True) + EPS) * g

    gq = gq_ref[...]
    q_scale = (HEAD_DIM ** -0.5) * math.log2(math.e)
    for h in range(N_HEADS):
        r0 = h * HEAD_DIM
        qh = rope(norm(pt[r0:r0 + HEAD_DIM], gq)) * q_scale
        qt_ref[0, r0:r0 + HEAD_DIM, :] = qh.astype(qt_ref.dtype)
        qih = rope(pt[D_A + 2 * HEAD_DIM + r0:D_A + 2 * HEAD_DIM + r0 + HEAD_DIM])
        qit_ref[0, r0:r0 + HEAD_DIM, :] = qih.astype(qit_ref.dtype)

    k_t = rope(norm(pt[D_A:D_A + HEAD_DIM], gk_ref[...]))
    v_t = pt[D_A + HEAD_DIM:D_A + 2 * HEAD_DIM]
    r_ki = 2 * D_A + 2 * HEAD_DIM
    ki_t = rope(pt[r_ki:r_ki + HEAD_DIM])
    wi_t = pt[r_ki + HEAD_DIM:r_ki + HEAD_DIM + N_HEADS]
    wit_ref[0] = wi_t * ((N_HEADS ** -0.5) * (HEAD_DIM ** -0.5))

    kk_t = jnp.concatenate([k_t, ki_t], axis=0)
    for c in range(tm // LANES):
        sl = slice(c * LANES, (c + 1) * LANES)
        kk_ref[0, sl, :] = kk_t[:, sl].T.astype(kk_ref.dtype)
        vt_ref[0, c] = v_t[:, sl].astype(vt_ref.dtype)


def _fold8(x, op):
    parts = [x[r:r + 8] for r in range(0, x.shape[0], 8)]
    while len(parts) > 1:
        nxt = [op(parts[a], parts[a + 1]) for a in range(0, len(parts) - 1, 2)]
        if len(parts) % 2:
            nxt.append(parts[-1])
        parts = nxt
    return parts[0]


def _loop_by_two(n, body, init):
    carry = lax.fori_loop(0, n // 2, lambda t, c: body(2 * t + 1, body(2 * t, c)), init)
    return lax.cond(n % 2 == 1, lambda c: body(n - 1, c), lambda c: c, carry)


def _loop_by_four(n, body, init):
    def quad(t, c):
        for u in range(4):
            c = body(4 * t + u, c)
        return c
    carry = lax.fori_loop(0, n // 4, quad, init)
    done = (n // 4) * 4
    carry = lax.cond(n - done >= 2, lambda c: body(done + 1, body(done, c)), lambda c: c, carry)
    return lax.cond(n % 2 == 1, lambda c: body(n - 1, c), lambda c: c, carry)


def _attn_kernel(qt_ref, qit_ref, wit_ref, kk_ref, vt_ref, o_ref, keys_scr, s_scr, acc_scr, *,
                 k_top, n_rows, ks):
    i = pl.program_id(1)
    nkb = ks // KB
    ns = i // nkb + 1
    n_keys = n_rows - PAD
    n_unseen = n_rows - ns * ks
    lane = lax.broadcasted_iota(I32, (1, QB), 1)
    sub = lax.broadcasted_iota(I32, (KB, QB), 0)
    q_row = i * QB + lane
    vis_end = jnp.where(q_row < LANES, LANES, ((q_row - LANES) // CHUNK + 1) * CHUNK + LANES)
    n_vis = vis_end - PAD

    zeros_half = jnp.zeros((HEAD_DIM, N_HEADS * QB), MXU_DTYPE)

    def stack_heads(ref):
        return jnp.concatenate([ref[0, h * HEAD_DIM:(h + 1) * HEAD_DIM, :] for h in range(N_HEADS)], axis=1)

    q_all = jnp.concatenate([stack_heads(qt_ref), zeros_half], axis=0)
    qi_all = jnp.concatenate([zeros_half, stack_heads(qit_ref)], axis=0)
    w = wit_ref[0]

    def score_body(j, carry):
        kmin, kmax = carry
        base = pl.multiple_of(j * ks, ks)
        kk_sb = kk_ref[0, pl.ds(base, ks), :]
        s = _dot(kk_sb, qi_all)
        score = jnp.maximum(s[:, :QB], 0.0) * w[0:1, :]
        for h in range(1, N_HEADS):
            score = score + jnp.maximum(s[:, h * QB:(h + 1) * QB], 0.0) * w[h:h + 1, :]
        for r in range(nkb):
            off = base + r * KB
            vis = sub < (vis_end - off)
            sc = jnp.where(vis, score[r * KB:(r + 1) * KB] + 0.0, NEG_INF)
            key = _sortable(lax.bitcast_convert_type(sc, I32))
            keys_scr[pl.ds(pl.multiple_of(off, KB), KB), :] = key
            kmax = jnp.maximum(kmax, _fold8(key, jnp.maximum))
            kmin = jnp.minimum(kmin, _fold8(jnp.where(vis, key, INT_MAX), jnp.minimum))
        return kmin, kmax

    kmin, kmax = _loop_by_four(ns, score_body,
                               (jnp.full((8, QB), INT_MAX, I32), jnp.full((8, QB), INT_MIN, I32)))
    keys_scr[0:PAD, :] = jnp.full((PAD, QB), INT_MIN, I32)
    kmin = jnp.min(kmin, axis=0, keepdims=True)
    kmax = jnp.max(kmax, axis=0, keepdims=True)

    @pl.when(ns % 2 == 1)
    def _():
        keys_scr[pl.ds(pl.multiple_of(ns * ks, ks), ks), :] = jnp.full((ks, QB), INT_MIN, I32)

    def reduce_keys(term, op, init):
        def body(t, acc):
            base = pl.multiple_of(t * (2 * ks), 2 * ks)
            for r in range(2 * nkb):
                kb = keys_scr[pl.ds(pl.multiple_of(base + r * KB, KB), KB), :]
                acc = op(acc, _fold8(term(kb), op))
            return acc
        return lax.fori_loop(0, (ns + 1) // 2, body, jnp.full((8, QB), init, I32))

    def count_ge(cand):
        c = reduce_keys(lambda kb: jnp.where(kb >= cand, 1, 0), jnp.add, 0)
        return jnp.sum(c, axis=0, keepdims=True) + jnp.where(_NEG_KEY >= cand, n_unseen, 0)

    few_visible = n_vis < k_top
    lo0 = jnp.where(few_visible, jnp.minimum(kmin, _NEG_KEY), kmin)
    c_lo0 = n_vis + jnp.where(_NEG_KEY >= lo0, n_keys - n_vis, 0)
    top = jnp.maximum(kmax, _NEG_KEY)
    hi0 = jnp.where(top == INT_MAX, INT_MAX, top + 1)

    def finished(lo, hi, c_lo, c_hi):
        return (c_lo == k_top) | (lo + 1 >= hi) | (c_hi == k_top - 1)

    def bisect_cond(st):
        p, lo, hi, c_lo, c_hi = st
        pending = jnp.where(finished(lo, hi, c_lo, c_hi), 0.0, 1.0)
        return (p < MAX_BISECT_STEPS) & (jnp.max(pending) > 0.0)

    def bisect_body(st):
        for _ in range(BISECT_STEPS_PER_TEST):
            st = bisect_step(st)
        return st

    def bisect_step(st):
        p, lo, hi, c_lo, c_hi = st
        done = finished(lo, hi, c_lo, c_hi)
        lo_f = lax.bitcast_convert_type(_sortable(lo), F32)
        hi_f = lax.bitcast_convert_type(_sortable(hi), F32)
        cand_f = _sortable(lax.bitcast_convert_type(lo_f * 0.5 + hi_f * 0.5, I32))
        cand_k = (lo >> 1) + (hi >> 1) + (lo & hi & 1)
        use_f = (cand_f > lo) & (cand_f < hi) & (p < FLOAT_MIDPOINT_STEPS)
        cand = jnp.where(use_f, cand_f, cand_k)
        cand = jnp.where((p == 0) & (lo < 0) & (hi > 0), 0, cand)
        cand = jnp.where((p == 1) & (lo == 0) & (hi > 1) & (c_lo - k_top <= ZERO_TIE_MARGIN), 1, cand)
        cand = jnp.where(few_visible & (p == 0), _NEG_KEY + 1, cand)
        cand = jnp.minimum(jnp.maximum(cand, lo + 1), hi - 1)
        c = count_ge(cand)
        up = (c >= k_top) & jnp.logical_not(done)
        dn = (c < k_top) & jnp.logical_not(done)
        return (p + 1, jnp.where(up, cand, lo), jnp.where(dn, cand, hi),
                jnp.where(up, c, c_lo), jnp.where(dn, c, c_hi))

    st = (jnp.int32(0), lo0, hi0, c_lo0, jnp.zeros((1, QB), I32))
    untested = jnp.where(i * QB > LANES + k_top, UNTESTED_BISECT_STEPS // BISECT_STEPS_PER_TEST, 0)
    st = lax.fori_loop(0, untested, lambda _, s: bisect_body(s), st)
    _, lo, hi, c_lo, c_hi = lax.while_loop(bisect_cond, bisect_body, st)

    below_hi = reduce_keys(lambda kb: jnp.where(kb < hi, kb, INT_MIN), jnp.maximum, INT_MIN)
    below_hi = jnp.max(below_hi, axis=0, keepdims=True)
    below_hi = jnp.where((n_unseen > 0) & (_NEG_KEY < hi), jnp.maximum(below_hi, _NEG_KEY), below_hi)
    exact_cut = c_lo == k_top
    tau = jnp.where(exact_cut | (lo + 1 >= hi), lo, below_hi)

    n_tied_wanted = jnp.where(exact_cut, n_rows, k_top - c_hi).astype(F32)
    row_ge_col = (lax.broadcasted_iota(I32, (KB, KB), 0) >= lax.broadcasted_iota(I32, (KB, KB), 1))
    prefix_ones = jnp.where(row_ge_col, 1.0, 0.0).astype(MXU_DTYPE)

    def logits_body(j, carry):
        m, tied_before = carry
        base = pl.multiple_of(j * ks, ks)
        bias = []
        for r in range(nkb):
            off = base + r * KB
            kb = keys_scr[pl.ds(pl.multiple_of(off, KB), KB), :]
            tied = jnp.where(kb == tau, 1.0, 0.0).astype(MXU_DTYPE)
            rank = _dot(prefix_ones, tied)
            admissible = rank + tied_before <= n_tied_wanted
            tied_before = tied_before + rank[KB - 1:KB, :]
            thr = tau - jnp.where(admissible, 1, 0)
            thr = jnp.where(sub < vis_end - off, thr, INT_MAX)
            bias.append(jnp.where(kb > thr, 0.0, NEG_INF))
        bias = jnp.concatenate(bias, axis=0)
        s = _dot(kk_ref[0, pl.ds(base, ks), :], q_all) + jnp.concatenate([bias] * N_HEADS, axis=1)
        s_scr[pl.ds(base, ks), :] = s
        return jnp.maximum(m, jnp.max(s, axis=0, keepdims=True)), tied_before

    m, _ = _loop_by_four(ns, logits_body,
                         (jnp.full((1, N_HEADS * QB), NEG_INF, F32), jnp.zeros((1, QB), F32)))

    acc_scr[...] = jnp.zeros_like(acc_scr)
    ones_rows = jnp.ones((V_ROWS - HEAD_DIM, ks), MXU_DTYPE)

    def value_body(j, carry):
        base = pl.multiple_of(j * ks, ks)
        p = jnp.exp2(s_scr[pl.ds(base, ks), :] - m).astype(MXU_DTYPE)
        v_ext = jnp.concatenate(
            [jnp.concatenate([vt_ref[0, j * nkb + r] for r in range(nkb)], axis=1), ones_rows], axis=0)
        acc_scr[...] += _dot(v_ext, p)
        return carry

    _loop_by_four(ns, value_body, 0)

    acc = acc_scr[...]
    out_t = acc[:HEAD_DIM] / acc[HEAD_DIM:HEAD_DIM + 1]
    for hp in range(N_HEADS // 2):
        blk = jnp.concatenate([out_t[:, (2 * hp) * QB:(2 * hp + 1) * QB],
                               out_t[:, (2 * hp + 1) * QB:(2 * hp + 2) * QB]], axis=0)
        o_ref[0, :, hp * LANES:(hp + 1) * LANES] = blk.T.astype(o_ref.dtype)


def _out_kernel(x_ref, attn_ref, rest_ref, halo_ref, band_ref, wa_ref, pw_ref, pb_ref, ps_ref, wb_ref,
                wo_ref, o_ref, *, tm):
    t = pl.program_id(1)
    row = t * tm + lax.broadcasted_iota(I32, (tm, LANES), 0)
    pos = row - PAD

    gate_a = rest_ref[0, :, 0:D_A].astype(F32)
    a = attn_ref[0].astype(F32) * (gate_a * jax.nn.sigmoid(gate_a))
    y_a = _dot(a.astype(MXU_DTYPE), wa_ref[...])

    u = rest_ref[0, :, D_A:D_A + D_B]
    halo = jnp.where(t > 0, halo_ref[0], jnp.zeros_like(halo_ref[0]))
    u_ext = jnp.concatenate([halo, u], axis=0)
    mixed = []
    for g, win in enumerate(POOL_WINDOWS):
        cs = slice(g * POOL_GROUP, (g + 1) * POOL_GROUP)
        window_sum = _dot(band_ref[g], u_ext[:, cs])
        cnt = jnp.clip(pos + 1, 1, win).astype(F32)
        pooled = window_sum / cnt - u[:, cs].astype(F32)
        mixed.append(_dot(pooled.astype(MXU_DTYPE), pw_ref[g]))
    mixed = (jnp.concatenate(mixed, axis=1) + pb_ref[...]) * ps_ref[...]
    gate_b = rest_ref[0, :, D_A + D_B:D_A + 2 * D_B].astype(F32)
    z_b = mixed * (gate_b * jax.nn.sigmoid(gate_b))
    y_b = _dot(z_b.astype(MXU_DTYPE), wb_ref[...])

    c0 = D_A + 2 * D_B
    g_a = jax.nn.sigmoid(rest_ref[0, :, c0:c0 + D_MODEL].astype(F32))
    g_b = jax.nn.sigmoid(rest_ref[0, :, c0 + D_MODEL:c0 + 2 * D_MODEL].astype(F32))
    mix = g_a * y_a + g_b * y_b
    out = x_ref[0] + _dot(mix.astype(MXU_DTYPE), wo_ref[...])
    row_full = t * tm + lax.broadcasted_iota(I32, (tm, D_MODEL), 0)
    o_ref[0] = jnp.where(row_full >= PAD, out, 0.0)


def _row_tile(n_rows):
    for tm in (384, 256, 128):
        if n_rows % tm == 0:
            return tm
    raise ValueError(f"sequence rows {n_rows} must be a multiple of 128")


def _key_step(n_rows):
    return 3 * KB if n_rows % (3 * KB) == 0 else KB


def _params(sem):
    return pltpu.CompilerParams(dimension_semantics=sem, vmem_limit_bytes=VMEM_LIMIT)


def _const_spec(shape):
    return pl.BlockSpec(shape, lambda b, t: (0,) * len(shape))


def _layer(h, lw, cos_t, sin_t, band, *, k_top):
    B, n_rows, _ = h.shape
    tm = _row_tile(n_rows)
    nt = n_rows // tm
    nblk = n_rows // LANES
    act = MXU_DTYPE

    qt, qit, wit, kk, vt, rest = pl.pallas_call(
        functools.partial(_proj_kernel, tm=tm),
        grid=(B, nt),
        in_specs=[
            pl.BlockSpec((1, tm, D_MODEL), lambda b, t: (b, t, 0)),
            _const_spec((1, D_MODEL)),
            _const_spec((N_T_ROWS, D_MODEL)),
            _const_spec((D_MODEL, D_REST)),
            _const_spec((HEAD_DIM, tm)),
            _const_spec((HEAD_DIM, tm)),
            pl.BlockSpec((HEAD_DIM // 2, tm), lambda b, t: (0, t)),
            pl.BlockSpec((HEAD_DIM // 2, tm), lambda b, t: (0, t)),
        ],
        out_specs=[
            pl.BlockSpec((1, D_A, tm), lambda b, t: (b, 0, t)),
            pl.BlockSpec((1, D_A, tm), lambda b, t: (b, 0, t)),
            pl.BlockSpec((1, N_HEADS, tm), lambda b, t: (b, 0, t)),
            pl.BlockSpec((1, tm, LANES), lambda b, t: (b, t, 0)),
            pl.BlockSpec((1, tm // LANES, HEAD_DIM, LANES), lambda b, t: (b, t, 0, 0)),
            pl.BlockSpec((1, tm, D_REST), lambda b, t: (b, t, 0)),
        ],
        out_shape=[
            jax.ShapeDtypeStruct((B, D_A, n_rows), act),
            jax.ShapeDtypeStruct((B, D_A, n_rows), act),
            jax.ShapeDtypeStruct((B, N_HEADS, n_rows), F32),
            jax.ShapeDtypeStruct((B, n_rows, LANES), act),
            jax.ShapeDtypeStruct((B, nblk, HEAD_DIM, LANES), act),
            jax.ShapeDtypeStruct((B, n_rows, D_REST), act),
        ],
        compiler_params=_params(("parallel", "parallel")),
        name="proj",
    )(h, lw["g"], lw["wt"], lw["ws"], lw["gq"], lw["gk"], cos_t, sin_t)

    assert n_rows - PAD >= k_top
    ks = _key_step(n_rows)
    attn = pl.pallas_call(
        functools.partial(_attn_kernel, k_top=k_top, n_rows=n_rows, ks=ks),
        grid=(B, nblk),
        in_specs=[
            pl.BlockSpec((1, D_A, QB), lambda b, i: (b, 0, i)),
            pl.BlockSpec((1, D_A, QB), lambda b, i: (b, 0, i)),
            pl.BlockSpec((1, N_HEADS, QB), lambda b, i: (b, 0, i)),
            pl.BlockSpec((1, n_rows, LANES), lambda b, i: (b, 0, 0)),
            pl.BlockSpec((1, nblk, HEAD_DIM, LANES), lambda b, i: (b, 0, 0, 0)),
        ],
        out_specs=pl.BlockSpec((1, QB, D_A), lambda b, i: (b, i, 0)),
        out_shape=jax.ShapeDtypeStruct((B, n_rows, D_A), act),
        scratch_shapes=[
            pltpu.VMEM((n_rows + ks, QB), I32),
            pltpu.VMEM((n_rows, N_HEADS * QB), F32),
            pltpu.VMEM((V_ROWS, N_HEADS * QB), F32),
        ],
        compiler_params=_params(("parallel", "arbitrary")),
        name="attn",
    )(qt, qit, wit, kk, vt)

    halo_blocks = tm // LANES
    out = pl.pallas_call(
        functools.partial(_out_kernel, tm=tm),
        grid=(B, nt),
        in_specs=[
            pl.BlockSpec((1, tm, D_MODEL), lambda b, t: (b, t, 0)),
            pl.BlockSpec((1, tm, D_A), lambda b, t: (b, t, 0)),
            pl.BlockSpec((1, tm, D_REST), lambda b, t: (b, t, 0)),
            pl.BlockSpec((1, LANES, D_B), lambda b, t: (b, jnp.maximum(t * halo_blocks - 1, 0), 1)),
            _const_spec((len(POOL_WINDOWS), tm, LANES + tm)),
            _const_spec((D_A, D_MODEL)),
            _const_spec((len(POOL_WINDOWS), POOL_GROUP, POOL_GROUP)),
            _const_spec((1, D_B)),
            _const_spec((1, D_B)),
            _const_spec((D_B, D_MODEL)),
            _const_spec((D_MODEL, D_MODEL)),
        ],
        out_specs=pl.BlockSpec((1, tm, D_MODEL), lambda b, t: (b, t, 0)),
        out_shape=jax.ShapeDtypeStruct((B, n_rows, D_MODEL), F32),
        compiler_params=_params(("parallel", "parallel")),
        name="outproj",
    )(h, attn, rest, rest, band, lw["wa"], lw["pw"], lw["pb"], lw["ps"], lw["wb"], lw["wo"])
    return out


def _prep_layer_weights(l, tm, norm_gain, w_in, q_norm_gain, k_norm_gain, pool_w, pool_b, pool_scale,
                        w_branch_a, w_branch_b, w_out):
    w = w_in[l]
    o_q, o_ga = 0, D_A + 2 * HEAD_DIM
    o_ub = o_ga + D_A
    o_gb = o_ub + D_B
    o_qi = o_gb + D_B
    o_ki = o_qi + D_A
    o_wi = o_ki + HEAD_DIM
    o_mg = o_wi + N_HEADS
    w_t = jnp.concatenate([w[:, o_q:o_ga], w[:, o_qi:o_mg],
                           jnp.zeros((D_MODEL, N_T_ROWS - (o_ga - o_q) - (o_mg - o_qi)), w.dtype)], axis=1).T
    w_s = jnp.concatenate([w[:, o_ga:o_qi], w[:, o_mg:]], axis=1)
    return {
        "g": norm_gain[l].reshape(1, D_MODEL).astype(F32),
        "wt": w_t.astype(MXU_DTYPE),
        "ws": w_s.astype(MXU_DTYPE),
        "gq": jnp.broadcast_to(q_norm_gain[l].astype(F32)[:, None], (HEAD_DIM, tm)),
        "gk": jnp.broadcast_to(k_norm_gain[l].astype(F32)[:, None], (HEAD_DIM, tm)),
        "wa": w_branch_a[l].astype(MXU_DTYPE),
        "pw": pool_w[l].astype(MXU_DTYPE),
        "pb": pool_b[l].reshape(1, D_B).astype(F32),
        "ps": pool_scale[l].reshape(1, D_B).astype(F32),
        "wb": w_branch_b[l].astype(MXU_DTYPE),
        "wo": w_out[l].astype(MXU_DTYPE),
    }


@jax.jit
def kernel(x, meta_tokens, norm_gain, w_in, q_norm_gain, k_norm_gain, pool_w, pool_b, pool_scale,
           w_branch_a, w_branch_b, w_out):
    B, S, _ = x.shape
    assert S % LANES == 0, "sequence length must be a multiple of 128"
    k_top = min(TOPK_MAX, S // 4)
    n_rows = LANES + S
    tm = _row_tile(n_rows)

    meta = jnp.broadcast_to(meta_tokens.astype(x.dtype)[None], (B, N_META, D_MODEL))
    h = jnp.concatenate([jnp.zeros((B, PAD, D_MODEL), x.dtype), meta, x], axis=1)

    pos = jnp.maximum(jnp.arange(n_rows, dtype=F32) - PAD, 0.0)
    inv_freq = 1.0 / (ROPE_THETA ** (jnp.arange(0, HEAD_DIM, 2, dtype=F32) / HEAD_DIM))
    ang = inv_freq[:, None] * pos[None, :]
    cos_t, sin_t = jnp.cos(ang), jnp.sin(ang)

    r = jnp.arange(tm)[:, None]
    c = jnp.arange(LANES + tm)[None, :] - LANES
    band = jnp.stack([((c <= r) & (c > r - w)) for w in POOL_WINDOWS]).astype(MXU_DTYPE)

    for l in range(DEPTH):
        lw = _prep_layer_weights(l, tm, norm_gain, w_in, q_norm_gain, k_norm_gain, pool_w, pool_b, pool_scale,
                                 w_branch_a, w_branch_b, w_out)
        h = _layer(h, lw, cos_t, sin_t, band, k_top=k_top)
    return h[:, LANES:]
```

```python
import functools
import math

import jax
import jax.numpy as jnp
import numpy as np
from jax import lax
from jax.experimental import pallas as pl
from jax.experimental.pallas import tpu as pltpu

D_MODEL = 1024
DEPTH = 4
CHUNK = 64
N_META = 16
N_HEADS = 8
HEAD_DIM = 64
D_A = N_HEADS * HEAD_DIM
D_B = 512
POOL_WINDOWS = (2, 4, 8, 16)
POOL_GROUP = 128
TOPK_MAX = 256
ROPE_THETA = 10000.0
EPS = 1e-6
NEG_INF = -1e30

LANES = 128
PAD = LANES - N_META
KB = LANES
QB = LANES
V_ROWS = HEAD_DIM + 16
N_T_ROWS = 2 * D_A + 3 * HEAD_DIM + 16
D_REST = D_A + D_B + D_B + 2 * D_MODEL
INT_MIN = -2 ** 31
INT_MAX = 2 ** 31 - 1
FLOAT_MIDPOINT_STEPS = 24
BISECT_STEPS_PER_TEST = 4
UNTESTED_BISECT_STEPS = 16
ZERO_TIE_MARGIN = 128
MAX_BISECT_STEPS = FLOAT_MIDPOINT_STEPS + 36
VMEM_LIMIT = 48 * 1024 * 1024

MXU_DTYPE = jnp.bfloat16

F32 = jnp.float32
I32 = jnp.int32


def _dot(a, b):
    return jnp.dot(a, b, preferred_element_type=F32)


def _sortable(bits):
    return bits ^ ((bits >> 31) & 0x7FFFFFFF)


_NEG_KEY = int(_sortable(np.float32(NEG_INF).view(np.int32)))


def _proj_kernel(x_ref, g_ref, wt_ref, ws_ref, gq_ref, gk_ref, cos_ref, sin_ref,
                 qt_ref, qit_ref, wit_ref, kk_ref, vt_ref, rest_ref, *, tm):
    x = x_ref[0]
    ms = jnp.mean(x * x, axis=-1, keepdims=True)
    hn = (x * lax.rsqrt(ms + EPS) * g_ref[...]).astype(MXU_DTYPE)

    for c in range(0, D_REST, 512):
        rest_ref[0, :, c:c + 512] = _dot(hn, ws_ref[:, c:c + 512]).astype(rest_ref.dtype)

    pt = lax.dot_general(wt_ref[...], hn, (((1,), (1,)), ((), ())), preferred_element_type=F32)
    cos = cos_ref[...]
    sin = sin_ref[...]
    half = HEAD_DIM // 2

    def rope(t):
        x1, x2 = t[:half], t[half:]
        return jnp.concatenate([x1 * cos - x2 * sin, x2 * cos + x1 * sin], axis=0)

    def norm(t, g):
        return t * lax.rsqrt(jnp.mean(t * t, axis=0, keepdims=True) + EPS) * g

    gq = gq_ref[...]
    q_scale = (HEAD_DIM ** -0.5) * math.log2(math.e)
    for h in range(N_HEADS):
        r0 = h * HEAD_DIM
        qh = rope(norm(pt[r0:r0 + HEAD_DIM], gq)) * q_scale
        qt_ref[0, r0:r0 + HEAD_DIM, :] = qh.astype(qt_ref.dtype)
        qih = rope(pt[D_A + 2 * HEAD_DIM + r0:D_A + 2 * HEAD_DIM + r0 + HEAD_DIM])
        qit_ref[0, r0:r0 + HEAD_DIM, :] = qih.astype(qit_ref.dtype)

    k_t = rope(norm(pt[D_A:D_A + HEAD_DIM], gk_ref[...]))
    v_t = pt[D_A + HEAD_DIM:D_A + 2 * HEAD_DIM]
    r_ki = 2 * D_A + 2 * HEAD_DIM
    ki_t = rope(pt[r_ki:r_ki + HEAD_DIM])
    wi_t = pt[r_ki + HEAD_DIM:r_ki + HEAD_DIM + N_HEADS]
    wit_ref[0] = wi_t * ((N_HEADS ** -0.5) * (HEAD_DIM ** -0.5))

    kk_t = jnp.concatenate([k_t, ki_t], axis=0)
    for c in range(tm // LANES):
        sl = slice(c * LANES, (c + 1) * LANES)
        kk_ref[0, sl, :] = kk_t[:, sl].T.astype(kk_ref.dtype)
        vt_ref[0, c] = v_t[:, sl].astype(vt_ref.dtype)


def _fold8(x, op):
    parts = [x[r:r + 8] for r in range(0, x.shape[0], 8)]
    while len(parts) > 1:
        nxt = [op(parts[a], parts[a + 1]) for a in range(0, len(parts) - 1, 2)]
        if len(parts) % 2:
            nxt.append(parts[-1])
        parts = nxt
    return parts[0]


def _loop_by_two(n, body, init):
    carry = lax.fori_loop(0, n // 2, lambda t, c: body(2 * t + 1, body(2 * t, c)), init)
    return lax.cond(n % 2 == 1, lambda c: body(n - 1, c), lambda c: c, carry)


def _loop_by_four(n, body, init):
    def quad(t, c):
        for u in range(4):
            c = body(4 * t + u, c)
        return c
    carry = lax.fori_loop(0, n // 4, quad, init)
    done = (n // 4) * 4
    carry = lax.cond(n - done >= 2, lambda c: body(done + 1, body(done, c)), lambda c: c, carry)
    return lax.cond(n % 2 == 1, lambda c: body(n - 1, c), lambda c: c, carry)


def _attn_kernel(qt_ref, qit_next_ref, wit_next_ref, qit_first_ref, wit_first_ref, kk_ref, vt_ref, o_ref,
                 keys2_scr, minmax2_scr, s_scr, acc_scr, *, k_top, n_rows, ks):
    i = pl.program_id(1)
    n_blocks = n_rows // QB
    nkb = ks // KB
    keys_scr = keys2_scr.at[i % 2]
    minmax_scr = minmax2_scr.at[i % 2]
    keys_next_scr = keys2_scr.at[(i + 1) % 2]
    minmax_next_scr = minmax2_scr.at[(i + 1) % 2]
    n_keys = n_rows - PAD
    lane = lax.broadcasted_iota(I32, (1, QB), 1)
    sub = lax.broadcasted_iota(I32, (KB, QB), 0)

    def block_geometry(t):
        q_row = t * QB + lane
        end = jnp.where(q_row < LANES, LANES, ((q_row - LANES) // CHUNK + 1) * CHUNK + LANES)
        return end, t // nkb + 1

    vis_end, ns = block_geometry(i)
    n_unseen = n_rows - ns * ks
    n_vis = vis_end - PAD

    zeros_half = jnp.zeros((HEAD_DIM, N_HEADS * QB), MXU_DTYPE)

    def stack_heads(ref):
        return jnp.concatenate([ref[0, h * HEAD_DIM:(h + 1) * HEAD_DIM, :] for h in range(N_HEADS)], axis=1)

    q_all = jnp.concatenate([stack_heads(qt_ref), zeros_half], axis=0)

    def score_step(j, carry, qi_all, w, end, dst_scr):
        kmin, kmax = carry
        base = pl.multiple_of(j * ks, ks)
        kk_sb = kk_ref[0, pl.ds(base, ks), :]
        s = _dot(kk_sb, qi_all)
        score = jnp.maximum(s[:, :QB], 0.0) * w[0:1, :]
        for h in range(1, N_HEADS):
            score = score + jnp.maximum(s[:, h * QB:(h + 1) * QB], 0.0) * w[h:h + 1, :]
        for r in range(nkb):
            off = base + r * KB
            vis = sub < (end - off)
            sc = jnp.where(vis, score[r * KB:(r + 1) * KB] + 0.0, NEG_INF)
            key = _sortable(lax.bitcast_convert_type(sc, I32))
            dst_scr[pl.ds(pl.multiple_of(off, KB), KB), :] = key
            kmax = jnp.maximum(kmax, _fold8(key, jnp.maximum))
            kmin = jnp.minimum(kmin, _fold8(jnp.where(vis, key, INT_MAX), jnp.minimum))
        return kmin, kmax

    no_minmax = (jnp.full((8, QB), INT_MAX, I32), jnp.full((8, QB), INT_MIN, I32))

    def indexer_operands(qit_ref, wit_ref):
        return jnp.concatenate([zeros_half, stack_heads(qit_ref)], axis=0), wit_ref[0]

    @pl.when(i == 0)
    def _():
        qi_all, w = indexer_operands(qit_first_ref, wit_first_ref)
        kmin, kmax = score_step(0, no_minmax, qi_all, w, vis_end, keys_scr)
        minmax_scr[0:8, :] = kmin
        minmax_scr[8:16, :] = kmax

    keys_scr[0:PAD, :] = jnp.full((PAD, QB), INT_MIN, I32)
    kmin = jnp.min(minmax_scr[0:8, :], axis=0, keepdims=True)
    kmax = jnp.max(minmax_scr[8:16, :], axis=0, keepdims=True)

    @pl.when(ns % 2 == 1)
    def _():
        keys_scr[pl.ds(pl.multiple_of(ns * ks, ks), ks), :] = jnp.full((ks, QB), INT_MIN, I32)

    def reduce_keys(term, op, init):
        def body(t, acc):
            base = pl.multiple_of(t * (2 * ks), 2 * ks)
            for r in range(2 * nkb):
                kb = keys_scr[pl.ds(pl.multiple_of(base + r * KB, KB), KB), :]
                acc = op(acc, _fold8(term(kb), op))
            return acc
        return lax.fori_loop(0, (ns + 1) // 2, body, jnp.full((8, QB), init, I32))

    def count_ge(cand):
        c = reduce_keys(lambda kb: jnp.where(kb >= cand, 1, 0), jnp.add, 0)
        return jnp.sum(c, axis=0, keepdims=True) + jnp.where(_NEG_KEY >= cand, n_unseen, 0)

    few_visible = n_vis < k_top
    lo0 = jnp.where(few_visible, jnp.minimum(kmin, _NEG_KEY), kmin)
    c_lo0 = n_vis + jnp.where(_NEG_KEY >= lo0, n_keys - n_vis, 0)
    top = jnp.maximum(kmax, _NEG_KEY)
    hi0 = jnp.where(top == INT_MAX, INT_MAX, top + 1)

    def finished(lo, hi, c_lo, c_hi):
        return (c_lo == k_top) | (lo + 1 >= hi) | (c_hi == k_top - 1)

    def bisect_cond(st):
        p, lo, hi, c_lo, c_hi = st
        pending = jnp.where(finished(lo, hi, c_lo, c_hi), 0.0, 1.0)
        return (p < MAX_BISECT_STEPS) & (jnp.max(pending) > 0.0)

    def bisect_body(st):
        for _ in range(BISECT_STEPS_PER_TEST):
            st = bisect_step(st)
        return st

    def bisect_step(st):
        p, lo, hi, c_lo, c_hi = st
        done = finished(lo, hi, c_lo, c_hi)
        lo_f = lax.bitcast_convert_type(_sortable(lo), F32)
        hi_f = lax.bitcast_convert_type(_sortable(hi), F32)
        cand_f = _sortable(lax.bitcast_convert_type(lo_f * 0.5 + hi_f * 0.5, I32))
        cand_k = (lo >> 1) + (hi >> 1) + (lo & hi & 1)
        use_f = (cand_f > lo) & (cand_f < hi) & (p < FLOAT_MIDPOINT_STEPS)
        cand = jnp.where(use_f, cand_f, cand_k)
        cand = jnp.where((p == 0) & (lo < 0) & (hi > 0), 0, cand)
        cand = jnp.where((p == 1) & (lo == 0) & (hi > 1) & (c_lo - k_top <= ZERO_TIE_MARGIN), 1, cand)
        cand = jnp.where(few_visible & (p == 0), _NEG_KEY + 1, cand)
        cand = jnp.minimum(jnp.maximum(cand, lo + 1), hi - 1)
        c = count_ge(cand)
        up = (c >= k_top) & jnp.logical_not(done)
        dn = (c < k_top) & jnp.logical_not(done)
        return (p + 1, jnp.where(up, cand, lo), jnp.where(dn, cand, hi),
                jnp.where(up, c, c_lo), jnp.where(dn, c, c_hi))

    st = (jnp.int32(0), lo0, hi0, c_lo0, jnp.zeros((1, QB), I32))
    untested = jnp.where(i * QB > LANES + k_top,
                         jnp.where(i * 3 > n_blocks, UNTESTED_BISECT_STEPS, UNTESTED_BISECT_STEPS - 4), 0)
    untested = untested // BISECT_STEPS_PER_TEST
    st = lax.fori_loop(0, untested, lambda _, s: bisect_body(s), st)
    _, lo, hi, c_lo, c_hi = lax.while_loop(bisect_cond, bisect_body, st)

    below_hi = reduce_keys(lambda kb: jnp.where(kb < hi, kb, INT_MIN), jnp.maximum, INT_MIN)
    below_hi = jnp.max(below_hi, axis=0, keepdims=True)
    below_hi = jnp.where((n_unseen > 0) & (_NEG_KEY < hi), jnp.maximum(below_hi, _NEG_KEY), below_hi)
    exact_cut = c_lo == k_top
    tau = jnp.where(exact_cut | (lo + 1 >= hi), lo, below_hi)

    n_tied_wanted = jnp.where(exact_cut, n_rows, k_top - c_hi).astype(F32)
    row_ge_col = (lax.broadcasted_iota(I32, (KB, KB), 0) >= lax.broadcasted_iota(I32, (KB, KB), 1))
    prefix_ones = jnp.where(row_ge_col, 1.0, 0.0).astype(MXU_DTYPE)

    def logits_body(j, carry):
        m, tied_before = carry
        base = pl.multiple_of(j * ks, ks)
        bias = []
        for r in range(nkb):
            off = base + r * KB
            kb = keys_scr[pl.ds(pl.multiple_of(off, KB), KB), :]
            tied = jnp.where(kb == tau, 1.0, 0.0).astype(MXU_DTYPE)
            rank = _dot(prefix_ones, tied)
            admissible = rank + tied_before <= n_tied_wanted
            tied_before = tied_before + rank[KB - 1:KB, :]
            thr = tau - jnp.where(admissible, 1, 0)
            thr = jnp.where(sub < vis_end - off, thr, INT_MAX)
            bias.append(jnp.where(kb > thr, 0.0, NEG_INF))
        bias = jnp.concatenate(bias, axis=0)
        s = _dot(kk_ref[0, pl.ds(base, ks), :], q_all) + jnp.concatenate([bias] * N_HEADS, axis=1)
        s_scr[pl.ds(base, ks), :] = s
        return jnp.maximum(m, jnp.max(s, axis=0, keepdims=True)), tied_before

    m, _ = _loop_by_four(ns, logits_body,
                         (jnp.full((1, N_HEADS * QB), NEG_INF, F32), jnp.zeros((1, QB), F32)))

    acc_scr[...] = jnp.zeros_like(acc_scr)
    ones_rows = jnp.ones((V_ROWS - HEAD_DIM, ks), MXU_DTYPE)

    def value_body(j, carry):
        base = pl.multiple_of(j * ks, ks)
        p = jnp.exp2(s_scr[pl.ds(base, ks), :] - m).astype(MXU_DTYPE)
        v_ext = jnp.concatenate(
            [jnp.concatenate([vt_ref[0, j * nkb + r] for r in range(nkb)], axis=1), ones_rows], axis=0)
        acc_scr[...] += _dot(v_ext, p)
        return carry

    @pl.when(i + 1 < n_blocks)
    def _():
        end_next, ns_next = block_geometry(i + 1)
        qi_all, w = indexer_operands(qit_next_ref, wit_next_ref)

        def both(j, carry):
            value_body(j, 0)
            return score_step(j, carry, qi_all, w, end_next, keys_next_scr)

        carry = _loop_by_four(ns, both, no_minmax)
        carry = lax.cond(ns_next > ns,
                         lambda c: score_step(ns, c, qi_all, w, end_next, keys_next_scr), lambda c: c, carry)
        minmax_next_scr[0:8, :] = carry[0]
        minmax_next_scr[8:16, :] = carry[1]

    @pl.when(i + 1 == n_blocks)
    def _():
        _loop_by_four(ns, value_body, 0)

    acc = acc_scr[...]
    out_t = acc[:HEAD_DIM] / acc[HEAD_DIM:HEAD_DIM + 1]
    for hp in range(N_HEADS // 2):
        blk = jnp.concatenate([out_t[:, (2 * hp) * QB:(2 * hp + 1) * QB],
                               out_t[:, (2 * hp + 1) * QB:(2 * hp + 2) * QB]], axis=0)
        o_ref[0, :, hp * LANES:(hp + 1) * LANES] = blk.T.astype(o_ref.dtype)


def _out_kernel(x_ref, attn_ref, rest_ref, halo_ref, band_ref, wa_ref, pw_ref, pb_ref, ps_ref, wb_ref,
                wo_ref, o_ref, *, tm):
    t = pl.program_id(1)
    row = t * tm + lax.broadcasted_iota(I32, (tm, LANES), 0)
    pos = row - PAD

    gate_a = rest_ref[0, :, 0:D_A].astype(F32)
    a = attn_ref[0].astype(F32) * (gate_a * jax.nn.sigmoid(gate_a))
    y_a = _dot(a.astype(MXU_DTYPE), wa_ref[...])

    u = rest_ref[0, :, D_A:D_A + D_B]
    halo = jnp.where(t > 0, halo_ref[0], jnp.zeros_like(halo_ref[0]))
    u_ext = jnp.concatenate([halo, u], axis=0)
    mixed = []
    for g, win in enumerate(POOL_WINDOWS):
        cs = slice(g * POOL_GROUP, (g + 1) * POOL_GROUP)
        window_sum = _dot(band_ref[g], u_ext[:, cs])
        cnt = jnp.clip(pos + 1, 1, win).astype(F32)
        pooled = window_sum / cnt - u[:, cs].astype(F32)
        mixed.append(_dot(pooled.astype(MXU_DTYPE), pw_ref[g]))
    mixed = (jnp.concatenate(mixed, axis=1) + pb_ref[...]) * ps_ref[...]
    gate_b = rest_ref[0, :, D_A + D_B:D_A + 2 * D_B].astype(F32)
    z_b = mixed * (gate_b * jax.nn.sigmoid(gate_b))
    y_b = _dot(z_b.astype(MXU_DTYPE), wb_ref[...])

    c0 = D_A + 2 * D_B
    g_a = jax.nn.sigmoid(rest_ref[0, :, c0:c0 + D_MODEL].astype(F32))
    g_b = jax.nn.sigmoid(rest_ref[0, :, c0 + D_MODEL:c0 + 2 * D_MODEL].astype(F32))
    mix = g_a * y_a + g_b * y_b
    out = x_ref[0] + _dot(mix.astype(MXU_DTYPE), wo_ref[...])
    row_full = t * tm + lax.broadcasted_iota(I32, (tm, D_MODEL), 0)
    o_ref[0] = jnp.where(row_full >= PAD, out, 0.0)


def _row_tile(n_rows):
    for tm in (384, 256, 128):
        if n_rows % tm == 0:
            return tm
    raise ValueError(f"sequence rows {n_rows} must be a multiple of 128")


def _key_step(n_rows):
    return 3 * KB if n_rows % (3 * KB) == 0 else KB


def _params(sem):
    return pltpu.CompilerParams(dimension_semantics=sem, vmem_limit_bytes=VMEM_LIMIT)


def _const_spec(shape):
    return pl.BlockSpec(shape, lambda b, t: (0,) * len(shape))


def _layer(h, lw, cos_t, sin_t, band, *, k_top):
    B, n_rows, _ = h.shape
    tm = _row_tile(n_rows)
    nt = n_rows // tm
    nblk = n_rows // LANES
    act = MXU_DTYPE

    qt, qit, wit, kk, vt, rest = pl.pallas_call(
        functools.partial(_proj_kernel, tm=tm),
        grid=(B, nt),
        in_specs=[
            pl.BlockSpec((1, tm, D_MODEL), lambda b, t: (b, t, 0)),
            _const_spec((1, D_MODEL)),
            _const_spec((N_T_ROWS, D_MODEL)),
            _const_spec((D_MODEL, D_REST)),
            _const_spec((HEAD_DIM, tm)),
            _const_spec((HEAD_DIM, tm)),
            pl.BlockSpec((HEAD_DIM // 2, tm), lambda b, t: (0, t)),
            pl.BlockSpec((HEAD_DIM // 2, tm), lambda b, t: (0, t)),
        ],
        out_specs=[
            pl.BlockSpec((1, D_A, tm), lambda b, t: (b, 0, t)),
            pl.BlockSpec((1, D_A, tm), lambda b, t: (b, 0, t)),
            pl.BlockSpec((1, N_HEADS, tm), lambda b, t: (b, 0, t)),
            pl.BlockSpec((1, tm, LANES), lambda b, t: (b, t, 0)),
            pl.BlockSpec((1, tm // LANES, HEAD_DIM, LANES), lambda b, t: (b, t, 0, 0)),
            pl.BlockSpec((1, tm, D_REST), lambda b, t: (b, t, 0)),
        ],
        out_shape=[
            jax.ShapeDtypeStruct((B, D_A, n_rows), act),
            jax.ShapeDtypeStruct((B, D_A, n_rows), act),
            jax.ShapeDtypeStruct((B, N_HEADS, n_rows), F32),
            jax.ShapeDtypeStruct((B, n_rows, LANES), act),
            jax.ShapeDtypeStruct((B, nblk, HEAD_DIM, LANES), act),
            jax.ShapeDtypeStruct((B, n_rows, D_REST), act),
        ],
        compiler_params=_params(("parallel", "parallel")),
        name="proj",
    )(h, lw["g"], lw["wt"], lw["ws"], lw["gq"], lw["gk"], cos_t, sin_t)

    assert n_rows - PAD >= k_top
    ks = _key_step(n_rows)
    attn = pl.pallas_call(
        functools.partial(_attn_kernel, k_top=k_top, n_rows=n_rows, ks=ks),
        grid=(B, nblk),
        in_specs=[
            pl.BlockSpec((1, D_A, QB), lambda b, i: (b, 0, i)),
            pl.BlockSpec((1, D_A, QB), lambda b, i: (b, 0, jnp.minimum(i + 1, nblk - 1))),
            pl.BlockSpec((1, N_HEADS, QB), lambda b, i: (b, 0, jnp.minimum(i + 1, nblk - 1))),
            pl.BlockSpec((1, D_A, QB), lambda b, i: (b, 0, 0)),
            pl.BlockSpec((1, N_HEADS, QB), lambda b, i: (b, 0, 0)),
            pl.BlockSpec((1, n_rows, LANES), lambda b, i: (b, 0, 0)),
            pl.BlockSpec((1, nblk, HEAD_DIM, LANES), lambda b, i: (b, 0, 0, 0)),
        ],
        out_specs=pl.BlockSpec((1, QB, D_A), lambda b, i: (b, i, 0)),
        out_shape=jax.ShapeDtypeStruct((B, n_rows, D_A), act),
        scratch_shapes=[
            pltpu.VMEM((2, n_rows + ks, QB), I32),
            pltpu.VMEM((2, 16, QB), I32),
            pltpu.VMEM((n_rows, N_HEADS * QB), F32),
            pltpu.VMEM((V_ROWS, N_HEADS * QB), F32),
        ],
        compiler_params=_params(("arbitrary", "arbitrary")),
        name="attn",
    )(qt, qit, wit, qit, wit, kk, vt)

    halo_blocks = tm // LANES
    out = pl.pallas_call(
        functools.partial(_out_kernel, tm=tm),
        grid=(B, nt),
        in_specs=[
            pl.BlockSpec((1, tm, D_MODEL), lambda b, t: (b, t, 0)),
            pl.BlockSpec((1, tm, D_A), lambda b, t: (b, t, 0)),
            pl.BlockSpec((1, tm, D_REST), lambda b, t: (b, t, 0)),
            pl.BlockSpec((1, LANES, D_B), lambda b, t: (b, jnp.maximum(t * halo_blocks - 1, 0), 1)),
            _const_spec((len(POOL_WINDOWS), tm, LANES + tm)),
            _const_spec((D_A, D_MODEL)),
            _const_spec((len(POOL_WINDOWS), POOL_GROUP, POOL_GROUP)),
            _const_spec((1, D_B)),
            _const_spec((1, D_B)),
            _const_spec((D_B, D_MODEL)),
            _const_spec((D_MODEL, D_MODEL)),
        ],
        out_specs=pl.BlockSpec((1, tm, D_MODEL), lambda b, t: (b, t, 0)),
        out_shape=jax.ShapeDtypeStruct((B, n_rows, D_MODEL), F32),
        compiler_params=_params(("parallel", "parallel")),
        name="outproj",
    )(h, attn, rest, rest, band, lw["wa"], lw["pw"], lw["pb"], lw["ps"], lw["wb"], lw["wo"])
    return out


def _prep_layer_weights(l, tm, norm_gain, w_in, q_norm_gain, k_norm_gain, pool_w, pool_b, pool_scale,
                        w_branch_a, w_branch_b, w_out):
    w = w_in[l]
    o_q, o_ga = 0, D_A + 2 * HEAD_DIM
    o_ub = o_ga + D_A
    o_gb = o_ub + D_B
    o_qi = o_gb + D_B
    o_ki = o_qi + D_A
    o_wi = o_ki + HEAD_DIM
    o_mg = o_wi + N_HEADS
    w_t = jnp.concatenate([w[:, o_q:o_ga], w[:, o_qi:o_mg],
                           jnp.zeros((D_MODEL, N_T_ROWS - (o_ga - o_q) - (o_mg - o_qi)), w.dtype)], axis=1).T
    w_s = jnp.concatenate([w[:, o_ga:o_qi], w[:, o_mg:]], axis=1)
    return {
        "g": norm_gain[l].reshape(1, D_MODEL).astype(F32),
        "wt": w_t.astype(MXU_DTYPE),
        "ws": w_s.astype(MXU_DTYPE),
        "gq": jnp.broadcast_to(q_norm_gain[l].astype(F32)[:, None], (HEAD_DIM, tm)),
        "gk": jnp.broadcast_to(k_norm_gain[l].astype(F32)[:, None], (HEAD_DIM, tm)),
        "wa": w_branch_a[l].astype(MXU_DTYPE),
        "pw": pool_w[l].astype(MXU_DTYPE),
        "pb": pool_b[l].reshape(1, D_B).astype(F32),
        "ps": pool_scale[l].reshape(1, D_B).astype(F32),
        "wb": w_branch_b[l].astype(MXU_DTYPE),
        "wo": w_out[l].astype(MXU_DTYPE),
    }


@jax.jit
def kernel(x, meta_tokens, norm_gain, w_in, q_norm_gain, k_norm_gain, pool_w, pool_b, pool_scale,
           w_branch_a, w_branch_b, w_out):
    B, S, _ = x.shape
    assert S % LANES == 0, "sequence length must be a multiple of 128"
    k_top = min(TOPK_MAX, S // 4)
    n_rows = LANES + S
    tm = _row_tile(n_rows)

    meta = jnp.broadcast_to(meta_tokens.astype(x.dtype)[None], (B, N_META, D_MODEL))
    h = jnp.concatenate([jnp.zeros((B, PAD, D_MODEL), x.dtype), meta, x], axis=1)

    pos = jnp.maximum(jnp.arange(n_rows, dtype=F32) - PAD, 0.0)
    inv_freq = 1.0 / (ROPE_THETA ** (jnp.arange(0, HEAD_DIM, 2, dtype=F32) / HEAD_DIM))
    ang = inv_freq[:, None] * pos[None, :]
    cos_t, sin_t = jnp.cos(ang), jnp.sin(ang)

    r = jnp.arange(tm)[:, None]
    c = jnp.arange(LANES + tm)[None, :] - LANES
    band = jnp.stack([((c <= r) & (c > r - w)) for w in POOL_WINDOWS]).astype(MXU_DTYPE)

    for l in range(DEPTH):
        lw = _prep_layer_weights(l, tm, norm_gain, w_in, q_norm_gain, k_norm_gain, pool_w, pool_b, pool_scale,
                                 w_branch_a, w_branch_b, w_out)
        h = _layer(h, lw, cos_t, sin_t, band, k_top=k_top)
    return h[:, LANES:]
```

```python
import functools
import math

import jax
import jax.numpy as jnp
import numpy as np
from jax import lax
from jax.experimental import pallas as pl
from jax.experimental.pallas import tpu as pltpu

D_MODEL = 1024
DEPTH = 4
CHUNK = 64
N_META = 16
N_HEADS = 8
HEAD_DIM = 64
D_A = N_HEADS * HEAD_DIM
D_B = 512
POOL_WINDOWS = (2, 4, 8, 16)
POOL_GROUP = 128
TOPK_MAX = 256
ROPE_THETA = 10000.0
EPS = 1e-6
NEG_INF = -1e30

LANES = 128
PAD = LANES - N_META
KB = LANES
QB = LANES
V_ROWS = HEAD_DIM + 16
N_T_ROWS = 2 * D_A + 3 * HEAD_DIM + 16
D_REST = D_A + D_B + D_B + 2 * D_MODEL
INT_MIN = -2 ** 31
INT_MAX = 2 ** 31 - 1
FLOAT_MIDPOINT_STEPS = 24
BISECT_STEPS_PER_TEST = 4
UNTESTED_BISECT_STEPS = 16
ZERO_TIE_MARGIN = 128
MAX_BISECT_STEPS = FLOAT_MIDPOINT_STEPS + 36
VMEM_LIMIT = 48 * 1024 * 1024

MXU_DTYPE = jnp.bfloat16

F32 = jnp.float32
I32 = jnp.int32


def _dot(a, b):
    return jnp.dot(a, b, preferred_element_type=F32)


def _sortable(bits):
    return bits ^ ((bits >> 31) & 0x7FFFFFFF)


_NEG_KEY = int(_sortable(np.float32(NEG_INF).view(np.int32)))


def _proj_kernel(x_ref, g_ref, wt_ref, ws_ref, gq_ref, gk_ref, cos_ref, sin_ref,
                 qt_ref, qit_ref, wit_ref, kk_ref, vt_ref, rest_ref, *, tm):
    x = x_ref[0]
    ms = jnp.mean(x * x, axis=-1, keepdims=True)
    hn = (x * lax.rsqrt(ms + EPS) * g_ref[...]).astype(MXU_DTYPE)

    for c in range(0, D_REST, 512):
        rest_ref[0, :, c:c + 512] = _dot(hn, ws_ref[:, c:c + 512]).astype(rest_ref.dtype)

    pt = lax.dot_general(wt_ref[...], hn, (((1,), (1,)), ((), ())), preferred_element_type=F32)
    cos = cos_ref[...]
    sin = sin_ref[...]
    half = HEAD_DIM // 2

    def rope(t):
        x1, x2 = t[:half], t[half:]
        return jnp.concatenate([x1 * cos - x2 * sin, x2 * cos + x1 * sin], axis=0)

    def norm(t, g):
        return t * lax.rsqrt(jnp.mean(t * t, axis=0, keepdims=True) + EPS) * g

    gq = gq_ref[...]
    q_scale = (HEAD_DIM ** -0.5) * math.log2(math.e)
    for h in range(N_HEADS):
        r0 = h * HEAD_DIM
        qh = rope(norm(pt[r0:r0 + HEAD_DIM], gq)) * q_scale
        qt_ref[0, r0:r0 + HEAD_DIM, :] = qh.astype(qt_ref.dtype)
        qih = rope(pt[D_A + 2 * HEAD_DIM + r0:D_A + 2 * HEAD_DIM + r0 + HEAD_DIM])
        qit_ref[0, r0:r0 + HEAD_DIM, :] = qih.astype(qit_ref.dtype)

    k_t = rope(norm(pt[D_A:D_A + HEAD_DIM], gk_ref[...]))
    v_t = pt[D_A + HEAD_DIM:D_A + 2 * HEAD_DIM]
    r_ki = 2 * D_A + 2 * HEAD_DIM
    ki_t = rope(pt[r_ki:r_ki + HEAD_DIM])
    wi_t = pt[r_ki + HEAD_DIM:r_ki + HEAD_DIM + N_HEADS]
    wit_ref[0] = wi_t * ((N_HEADS ** -0.5) * (HEAD_DIM ** -0.5))

    kk_t = jnp.concatenate([k_t, ki_t], axis=0)
    for c in range(tm // LANES):
        sl = slice(c * LANES, (c + 1) * LANES)
        kk_ref[0, sl, :] = kk_t[:, sl].T.astype(kk_ref.dtype)
        vt_ref[0, c] = v_t[:, sl].astype(vt_ref.dtype)


def _fold8(x, op):
    parts = [x[r:r + 8] for r in range(0, x.shape[0], 8)]
    while len(parts) > 1:
        nxt = [op(parts[a], parts[a + 1]) for a in range(0, len(parts) - 1, 2)]
        if len(parts) % 2:
            nxt.append(parts[-1])
        parts = nxt
    return parts[0]


def _loop_by_two(n, body, init):
    carry = lax.fori_loop(0, n // 2, lambda t, c: body(2 * t + 1, body(2 * t, c)), init)
    return lax.cond(n % 2 == 1, lambda c: body(n - 1, c), lambda c: c, carry)


def _loop_by_four(n, body, init):
    def quad(t, c):
        for u in range(4):
            c = body(4 * t + u, c)
        return c
    carry = lax.fori_loop(0, n // 4, quad, init)
    done = (n // 4) * 4
    carry = lax.cond(n - done >= 2, lambda c: body(done + 1, body(done, c)), lambda c: c, carry)
    return lax.cond(n % 2 == 1, lambda c: body(n - 1, c), lambda c: c, carry)


def _attn_kernel(qt_ref, qit_next_ref, wit_next_ref, qit_first_ref, wit_first_ref, kk_ref, vt_ref, o_ref,
                 keys2_scr, minmax2_scr, s_scr, acc_scr, *, k_top, n_rows, ks):
    i = pl.program_id(1)
    n_blocks = n_rows // QB
    nkb = ks // KB
    keys_scr = keys2_scr.at[i % 2]
    minmax_scr = minmax2_scr.at[i % 2]
    keys_next_scr = keys2_scr.at[(i + 1) % 2]
    minmax_next_scr = minmax2_scr.at[(i + 1) % 2]
    n_keys = n_rows - PAD
    lane = lax.broadcasted_iota(I32, (1, QB), 1)
    sub = lax.broadcasted_iota(I32, (KB, QB), 0)

    def block_geometry(t):
        q_row = t * QB + lane
        end = jnp.where(q_row < LANES, LANES, ((q_row - LANES) // CHUNK + 1) * CHUNK + LANES)
        return end, t // nkb + 1

    vis_end, ns = block_geometry(i)
    n_unseen = n_rows - ns * ks
    n_vis = vis_end - PAD

    zeros_half = jnp.zeros((HEAD_DIM, N_HEADS * QB), MXU_DTYPE)

    def stack_heads(ref):
        return jnp.concatenate([ref[0, h * HEAD_DIM:(h + 1) * HEAD_DIM, :] for h in range(N_HEADS)], axis=1)

    q_all = jnp.concatenate([stack_heads(qt_ref), zeros_half], axis=0)

    def score_step(j, carry, qi_all, w, end, dst_scr):
        kmin, kmax = carry
        base = pl.multiple_of(j * ks, ks)
        kk_sb = kk_ref[0, pl.ds(base, ks), :]
        s = _dot(kk_sb, qi_all)
        score = jnp.maximum(s[:, :QB], 0.0) * w[0:1, :]
        for h in range(1, N_HEADS):
            score = score + jnp.maximum(s[:, h * QB:(h + 1) * QB], 0.0) * w[h:h + 1, :]
        for r in range(nkb):
            off = base + r * KB
            vis = sub < (end - off)
            sc = jnp.where(vis, score[r * KB:(r + 1) * KB] + 0.0, NEG_INF)
            key = _sortable(lax.bitcast_convert_type(sc, I32))
            dst_scr[pl.ds(pl.multiple_of(off, KB), KB), :] = key
            kmax = jnp.maximum(kmax, _fold8(key, jnp.maximum))
            kmin = jnp.minimum(kmin, _fold8(jnp.where(vis, key, INT_MAX), jnp.minimum))
        return kmin, kmax

    no_minmax = (jnp.full((8, QB), INT_MAX, I32), jnp.full((8, QB), INT_MIN, I32))

    def indexer_operands(qit_ref, wit_ref):
        return jnp.concatenate([zeros_half, stack_heads(qit_ref)], axis=0), wit_ref[0]

    @pl.when(i == 0)
    def _():
        qi_all, w = indexer_operands(qit_first_ref, wit_first_ref)
        kmin, kmax = score_step(0, no_minmax, qi_all, w, vis_end, keys_scr)
        minmax_scr[0:8, :] = kmin
        minmax_scr[8:16, :] = kmax

    keys_scr[0:PAD, :] = jnp.full((PAD, QB), INT_MIN, I32)
    kmin = jnp.min(minmax_scr[0:8, :], axis=0, keepdims=True)
    kmax = jnp.max(minmax_scr[8:16, :], axis=0, keepdims=True)

    @pl.when(ns % 2 == 1)
    def _():
        keys_scr[pl.ds(pl.multiple_of(ns * ks, ks), ks), :] = jnp.full((ks, QB), INT_MIN, I32)

    def reduce_keys(term, op, init):
        def body(t, acc):
            base = pl.multiple_of(t * (2 * ks), 2 * ks)
            for r in range(2 * nkb):
                kb = keys_scr[pl.ds(pl.multiple_of(base + r * KB, KB), KB), :]
                acc = op(acc, _fold8(term(kb), op))
            return acc
        return lax.fori_loop(0, (ns + 1) // 2, body, jnp.full((8, QB), init, I32))

    def count_ge(cand):
        c = reduce_keys(lambda kb: jnp.where(kb >= cand, 1, 0), jnp.add, 0)
        return jnp.sum(c, axis=0, keepdims=True) + jnp.where(_NEG_KEY >= cand, n_unseen, 0)

    few_visible = n_vis < k_top
    lo0 = jnp.where(few_visible, jnp.minimum(kmin, _NEG_KEY), kmin)
    c_lo0 = n_vis + jnp.where(_NEG_KEY >= lo0, n_keys - n_vis, 0)
    top = jnp.maximum(kmax, _NEG_KEY)
    hi0 = jnp.where(top == INT_MAX, INT_MAX, top + 1)

    def finished(lo, hi, c_lo, c_hi):
        return (c_lo == k_top) | (lo + 1 >= hi) | (c_hi == k_top - 1)

    def bisect_cond(st):
        p, lo, hi, c_lo, c_hi = st
        pending = jnp.where(finished(lo, hi, c_lo, c_hi), 0.0, 1.0)
        return (p < MAX_BISECT_STEPS) & (jnp.max(pending) > 0.0)

    def bisect_body(st):
        for _ in range(BISECT_STEPS_PER_TEST):
            st = bisect_step(st)
        return st

    def bisect_step(st):
        p, lo, hi, c_lo, c_hi = st
        done = finished(lo, hi, c_lo, c_hi)
        lo_f = lax.bitcast_convert_type(_sortable(lo), F32)
        hi_f = lax.bitcast_convert_type(_sortable(hi), F32)
        cand_f = _sortable(lax.bitcast_convert_type(lo_f * 0.5 + hi_f * 0.5, I32))
        cand_k = (lo >> 1) + (hi >> 1) + (lo & hi & 1)
        use_f = (cand_f > lo) & (cand_f < hi) & (p < FLOAT_MIDPOINT_STEPS)
        cand = jnp.where(use_f, cand_f, cand_k)
        cand = jnp.where((p == 0) & (lo < 0) & (hi > 0), 0, cand)
        cand = jnp.where((p == 1) & (lo == 0) & (hi > 1) & (c_lo - k_top <= ZERO_TIE_MARGIN), 1, cand)
        cand = jnp.where(few_visible & (p == 0), _NEG_KEY + 1, cand)
        cand = jnp.minimum(jnp.maximum(cand, lo + 1), hi - 1)
        c = count_ge(cand)
        up = (c >= k_top) & jnp.logical_not(done)
        dn = (c < k_top) & jnp.logical_not(done)
        return (p + 1, jnp.where(up, cand, lo), jnp.where(dn, cand, hi),
                jnp.where(up, c, c_lo), jnp.where(dn, c, c_hi))

    st = (jnp.int32(0), lo0, hi0, c_lo0, jnp.zeros((1, QB), I32))
    untested = jnp.where(i * QB > LANES + k_top,
                         jnp.where(i * 3 > n_blocks, UNTESTED_BISECT_STEPS, UNTESTED_BISECT_STEPS - 4), 0)
    untested = untested // BISECT_STEPS_PER_TEST
    st = lax.fori_loop(0, untested, lambda _, s: bisect_body(s), st)
    _, lo, hi, c_lo, c_hi = lax.while_loop(bisect_cond, bisect_body, st)

    below_hi = reduce_keys(lambda kb: jnp.where(kb < hi, kb, INT_MIN), jnp.maximum, INT_MIN)
    below_hi = jnp.max(below_hi, axis=0, keepdims=True)
    below_hi = jnp.where((n_unseen > 0) & (_NEG_KEY < hi), jnp.maximum(below_hi, _NEG_KEY), below_hi)
    exact_cut = c_lo == k_top
    tau = jnp.where(exact_cut | (lo + 1 >= hi), lo, below_hi)

    n_tied_wanted = jnp.where(exact_cut, n_rows, k_top - c_hi).astype(F32)
    row_ge_col = (lax.broadcasted_iota(I32, (KB, KB), 0) >= lax.broadcasted_iota(I32, (KB, KB), 1))
    prefix_ones = jnp.where(row_ge_col, 1.0, 0.0).astype(MXU_DTYPE)

    def logits_body(j, carry):
        m, tied_before = carry
        base = pl.multiple_of(j * ks, ks)
        bias = []
        for r in range(nkb):
            off = base + r * KB
            kb = keys_scr[pl.ds(pl.multiple_of(off, KB), KB), :]
            tied = jnp.where(kb == tau, 1.0, 0.0).astype(MXU_DTYPE)
            rank = _dot(prefix_ones, tied)
            admissible = rank + tied_before <= n_tied_wanted
            tied_before = tied_before + rank[KB - 1:KB, :]
            thr = tau - jnp.where(admissible, 1, 0)
            thr = jnp.where(sub < vis_end - off, thr, INT_MAX)
            bias.append(jnp.where(kb > thr, 0.0, NEG_INF))
        bias = jnp.concatenate(bias, axis=0)
        s = _dot(kk_ref[0, pl.ds(base, ks), :], q_all) + jnp.concatenate([bias] * N_HEADS, axis=1)
        s_scr[pl.ds(base, ks), :] = s
        return jnp.maximum(m, jnp.max(s, axis=0, keepdims=True)), tied_before

    m, _ = _loop_by_four(ns, logits_body,
                         (jnp.full((1, N_HEADS * QB), NEG_INF, F32), jnp.zeros((1, QB), F32)))

    acc_scr[...] = jnp.zeros_like(acc_scr)
    ones_rows = jnp.ones((V_ROWS - HEAD_DIM, ks), MXU_DTYPE)

    def value_body(j, carry):
        base = pl.multiple_of(j * ks, ks)
        p = jnp.exp2(s_scr[pl.ds(base, ks), :] - m).astype(MXU_DTYPE)
        v_ext = jnp.concatenate(
            [jnp.concatenate([vt_ref[0, j * nkb + r] for r in range(nkb)], axis=1), ones_rows], axis=0)
        acc_scr[...] += _dot(v_ext, p)
        return carry

    @pl.when(i + 1 < n_blocks)
    def _():
        end_next, ns_next = block_geometry(i + 1)
        qi_all, w = indexer_operands(qit_next_ref, wit_next_ref)

        def both(j, carry):
            value_body(j, 0)
            return score_step(j, carry, qi_all, w, end_next, keys_next_scr)

        carry = _loop_by_four(ns, both, no_minmax)
        carry = lax.cond(ns_next > ns,
                         lambda c: score_step(ns, c, qi_all, w, end_next, keys_next_scr), lambda c: c, carry)
        minmax_next_scr[0:8, :] = carry[0]
        minmax_next_scr[8:16, :] = carry[1]

    @pl.when(i + 1 == n_blocks)
    def _():
        _loop_by_four(ns, value_body, 0)

    acc = acc_scr[...]
    out_t = acc[:HEAD_DIM] / acc[HEAD_DIM:HEAD_DIM + 1]
    for hp in range(N_HEADS // 2):
        blk = jnp.concatenate([out_t[:, (2 * hp) * QB:(2 * hp + 1) * QB],
                               out_t[:, (2 * hp + 1) * QB:(2 * hp + 2) * QB]], axis=0)
        o_ref[0, :, hp * LANES:(hp + 1) * LANES] = blk.T.astype(o_ref.dtype)


def _out_kernel(x_ref, attn_ref, rest_ref, halo_ref, band_ref, wa_ref, pw_ref, pb_ref, ps_ref, wb_ref,
                wo_ref, o_ref, *, tm):
    t = pl.program_id(1)
    row = t * tm + lax.broadcasted_iota(I32, (tm, LANES), 0)
    pos = row - PAD

    gate_a = rest_ref[0, :, 0:D_A].astype(F32)
    a = attn_ref[0].astype(F32) * (gate_a * jax.nn.sigmoid(gate_a))
    y_a = _dot(a.astype(MXU_DTYPE), wa_ref[...])

    u = rest_ref[0, :, D_A:D_A + D_B]
    halo = jnp.where(t > 0, halo_ref[0], jnp.zeros_like(halo_ref[0]))
    u_ext = jnp.concatenate([halo, u], axis=0)
    n_chunks = tm // LANES
    pooled = []
    for g, win in enumerate(POOL_WINDOWS):
        cs = slice(g * POOL_GROUP, (g + 1) * POOL_GROUP)
        windows = jnp.concatenate([u_ext[c * LANES:(c + 2) * LANES, cs] for c in range(n_chunks)], axis=1)
        sums = _dot(band_ref[g], windows)
        window_sum = jnp.concatenate([sums[:, c * LANES:(c + 1) * LANES] for c in range(n_chunks)], axis=0)
        cnt = jnp.clip(pos + 1, 1, win).astype(F32)
        pooled.append((window_sum / cnt - u[:, cs].astype(F32)).astype(MXU_DTYPE))
    mixed = _dot(jnp.concatenate(pooled, axis=1), pw_ref[...])
    mixed = (mixed + pb_ref[...]) * ps_ref[...]
    gate_b = rest_ref[0, :, D_A + D_B:D_A + 2 * D_B].astype(F32)
    z_b = mixed * (gate_b * jax.nn.sigmoid(gate_b))
    y_b = _dot(z_b.astype(MXU_DTYPE), wb_ref[...])

    c0 = D_A + 2 * D_B
    g_a = jax.nn.sigmoid(rest_ref[0, :, c0:c0 + D_MODEL].astype(F32))
    g_b = jax.nn.sigmoid(rest_ref[0, :, c0 + D_MODEL:c0 + 2 * D_MODEL].astype(F32))
    mix = g_a * y_a + g_b * y_b
    out = x_ref[0] + _dot(mix.astype(MXU_DTYPE), wo_ref[...])
    row_full = t * tm + lax.broadcasted_iota(I32, (tm, D_MODEL), 0)
    o_ref[0] = jnp.where(row_full >= PAD, out, 0.0)


def _row_tile(n_rows):
    for tm in (384, 256, 128):
        if n_rows % tm == 0:
            return tm
    raise ValueError(f"sequence rows {n_rows} must be a multiple of 128")


def _key_step(n_rows):
    return 3 * KB if n_rows % (3 * KB) == 0 else KB


def _params(sem):
    return pltpu.CompilerParams(dimension_semantics=sem, vmem_limit_bytes=VMEM_LIMIT)


def _const_spec(shape):
    return pl.BlockSpec(shape, lambda b, t: (0,) * len(shape))


def _layer(h, lw, cos_t, sin_t, band, *, k_top):
    B, n_rows, _ = h.shape
    tm = _row_tile(n_rows)
    nt = n_rows // tm
    nblk = n_rows // LANES
    act = MXU_DTYPE

    qt, qit, wit, kk, vt, rest = pl.pallas_call(
        functools.partial(_proj_kernel, tm=tm),
        grid=(B, nt),
        in_specs=[
            pl.BlockSpec((1, tm, D_MODEL), lambda b, t: (b, t, 0)),
            _const_spec((1, D_MODEL)),
            _const_spec((N_T_ROWS, D_MODEL)),
            _const_spec((D_MODEL, D_REST)),
            _const_spec((HEAD_DIM, tm)),
            _const_spec((HEAD_DIM, tm)),
            pl.BlockSpec((HEAD_DIM // 2, tm), lambda b, t: (0, t)),
            pl.BlockSpec((HEAD_DIM // 2, tm), lambda b, t: (0, t)),
        ],
        out_specs=[
            pl.BlockSpec((1, D_A, tm), lambda b, t: (b, 0, t)),
            pl.BlockSpec((1, D_A, tm), lambda b, t: (b, 0, t)),
            pl.BlockSpec((1, N_HEADS, tm), lambda b, t: (b, 0, t)),
            pl.BlockSpec((1, tm, LANES), lambda b, t: (b, t, 0)),
            pl.BlockSpec((1, tm // LANES, HEAD_DIM, LANES), lambda b, t: (b, t, 0, 0)),
            pl.BlockSpec((1, tm, D_REST), lambda b, t: (b, t, 0)),
        ],
        out_shape=[
            jax.ShapeDtypeStruct((B, D_A, n_rows), act),
            jax.ShapeDtypeStruct((B, D_A, n_rows), act),
            jax.ShapeDtypeStruct((B, N_HEADS, n_rows), F32),
            jax.ShapeDtypeStruct((B, n_rows, LANES), act),
            jax.ShapeDtypeStruct((B, nblk, HEAD_DIM, LANES), act),
            jax.ShapeDtypeStruct((B, n_rows, D_REST), act),
        ],
        compiler_params=_params(("parallel", "parallel")),
        name="proj",
    )(h, lw["g"], lw["wt"], lw["ws"], lw["gq"], lw["gk"], cos_t, sin_t)

    assert n_rows - PAD >= k_top
    ks = _key_step(n_rows)
    attn = pl.pallas_call(
        functools.partial(_attn_kernel, k_top=k_top, n_rows=n_rows, ks=ks),
        grid=(B, nblk),
        in_specs=[
            pl.BlockSpec((1, D_A, QB), lambda b, i: (b, 0, i)),
            pl.BlockSpec((1, D_A, QB), lambda b, i: (b, 0, jnp.minimum(i + 1, nblk - 1))),
            pl.BlockSpec((1, N_HEADS, QB), lambda b, i: (b, 0, jnp.minimum(i + 1, nblk - 1))),
            pl.BlockSpec((1, D_A, QB), lambda b, i: (b, 0, 0)),
            pl.BlockSpec((1, N_HEADS, QB), lambda b, i: (b, 0, 0)),
            pl.BlockSpec((1, n_rows, LANES), lambda b, i: (b, 0, 0)),
            pl.BlockSpec((1, nblk, HEAD_DIM, LANES), lambda b, i: (b, 0, 0, 0)),
        ],
        out_specs=pl.BlockSpec((1, QB, D_A), lambda b, i: (b, i, 0)),
        out_shape=jax.ShapeDtypeStruct((B, n_rows, D_A), act),
        scratch_shapes=[
            pltpu.VMEM((2, n_rows + ks, QB), I32),
            pltpu.VMEM((2, 16, QB), I32),
            pltpu.VMEM((n_rows, N_HEADS * QB), F32),
            pltpu.VMEM((V_ROWS, N_HEADS * QB), F32),
        ],
        compiler_params=_params(("arbitrary", "arbitrary")),
        name="attn",
    )(qt, qit, wit, qit, wit, kk, vt)

    halo_blocks = tm // LANES
    out = pl.pallas_call(
        functools.partial(_out_kernel, tm=tm),
        grid=(B, nt),
        in_specs=[
            pl.BlockSpec((1, tm, D_MODEL), lambda b, t: (b, t, 0)),
            pl.BlockSpec((1, tm, D_A), lambda b, t: (b, t, 0)),
            pl.BlockSpec((1, tm, D_REST), lambda b, t: (b, t, 0)),
            pl.BlockSpec((1, LANES, D_B), lambda b, t: (b, jnp.maximum(t * halo_blocks - 1, 0), 1)),
            _const_spec((len(POOL_WINDOWS), LANES, 2 * LANES)),
            _const_spec((D_A, D_MODEL)),
            _const_spec((D_B, D_B)),
            _const_spec((1, D_B)),
            _const_spec((1, D_B)),
            _const_spec((D_B, D_MODEL)),
            _const_spec((D_MODEL, D_MODEL)),
        ],
        out_specs=pl.BlockSpec((1, tm, D_MODEL), lambda b, t: (b, t, 0)),
        out_shape=jax.ShapeDtypeStruct((B, n_rows, D_MODEL), F32),
        compiler_params=_params(("parallel", "parallel")),
        name="outproj",
    )(h, attn, rest, rest, band, lw["wa"], lw["pw"], lw["pb"], lw["ps"], lw["wb"], lw["wo"])
    return out


def _prep_layer_weights(l, tm, norm_gain, w_in, q_norm_gain, k_norm_gain, pool_w, pool_b, pool_scale,
                        w_branch_a, w_branch_b, w_out):
    w = w_in[l]
    o_q, o_ga = 0, D_A + 2 * HEAD_DIM
    o_ub = o_ga + D_A
    o_gb = o_ub + D_B
    o_qi = o_gb + D_B
    o_ki = o_qi + D_A
    o_wi = o_ki + HEAD_DIM
    o_mg = o_wi + N_HEADS
    w_t = jnp.concatenate([w[:, o_q:o_ga], w[:, o_qi:o_mg],
                           jnp.zeros((D_MODEL, N_T_ROWS - (o_ga - o_q) - (o_mg - o_qi)), w.dtype)], axis=1).T
    w_s = jnp.concatenate([w[:, o_ga:o_qi], w[:, o_mg:]], axis=1)
    return {
        "g": norm_gain[l].reshape(1, D_MODEL).astype(F32),
        "wt": w_t.astype(MXU_DTYPE),
        "ws": w_s.astype(MXU_DTYPE),
        "gq": jnp.broadcast_to(q_norm_gain[l].astype(F32)[:, None], (HEAD_DIM, tm)),
        "gk": jnp.broadcast_to(k_norm_gain[l].astype(F32)[:, None], (HEAD_DIM, tm)),
        "wa": w_branch_a[l].astype(MXU_DTYPE),
        "pw": jax.scipy.linalg.block_diag(*[pool_w[l, g] for g in range(len(POOL_WINDOWS))]).astype(MXU_DTYPE),
        "pb": pool_b[l].reshape(1, D_B).astype(F32),
        "ps": pool_scale[l].reshape(1, D_B).astype(F32),
        "wb": w_branch_b[l].astype(MXU_DTYPE),
        "wo": w_out[l].astype(MXU_DTYPE),
    }


@jax.jit
def kernel(x, meta_tokens, norm_gain, w_in, q_norm_gain, k_norm_gain, pool_w, pool_b, pool_scale,
           w_branch_a, w_branch_b, w_out):
    B, S, _ = x.shape
    assert S % LANES == 0, "sequence length must be a multiple of 128"
    k_top = min(TOPK_MAX, S // 4)
    n_rows = LANES + S
    tm = _row_tile(n_rows)

    meta = jnp.broadcast_to(meta_tokens.astype(x.dtype)[None], (B, N_META, D_MODEL))
    h = jnp.concatenate([jnp.zeros((B, PAD, D_MODEL), x.dtype), meta, x], axis=1)

    pos = jnp.maximum(jnp.arange(n_rows, dtype=F32) - PAD, 0.0)
    inv_freq = 1.0 / (ROPE_THETA ** (jnp.arange(0, HEAD_DIM, 2, dtype=F32) / HEAD_DIM))
    ang = inv_freq[:, None] * pos[None, :]
    cos_t, sin_t = jnp.cos(ang), jnp.sin(ang)

    r = jnp.arange(LANES)[:, None]
    c = jnp.arange(2 * LANES)[None, :] - LANES
    band = jnp.stack([((c <= r) & (c > r - w)) for w in POOL_WINDOWS]).astype(MXU_DTYPE)

    for l in range(DEPTH):
        lw = _prep_layer_weights(l, tm, norm_gain, w_in, q_norm_gain, k_norm_gain, pool_w, pool_b, pool_scale,
                                 w_branch_a, w_branch_b, w_out)
        h = _layer(h, lw, cos_t, sin_t, band, k_top=k_top)
    return h[:, LANES:]
```

```python
import functools
import math

import jax
import jax.numpy as jnp
import numpy as np
from jax import lax
from jax.experimental import pallas as pl
from jax.experimental.pallas import tpu as pltpu

D_MODEL = 1024
DEPTH = 4
CHUNK = 64
N_META = 16
N_HEADS = 8
HEAD_DIM = 64
D_A = N_HEADS * HEAD_DIM
D_B = 512
POOL_WINDOWS = (2, 4, 8, 16)
POOL_GROUP = 128
TOPK_MAX = 256
ROPE_THETA = 10000.0
EPS = 1e-6
NEG_INF = -1e30

LANES = 128
PAD = LANES - N_META
KB = LANES
QB = LANES
V_ROWS = HEAD_DIM + 16
N_T_ROWS = 2 * D_A + 3 * HEAD_DIM + 16
D_REST = D_A + D_B + D_B + 2 * D_MODEL
INT_MIN = -2 ** 31
INT_MAX = 2 ** 31 - 1
FLOAT_MIDPOINT_STEPS = 24
BISECT_STEPS_PER_TEST = 4
UNTESTED_BISECT_STEPS = 16
ZERO_TIE_MARGIN = 128
MAX_BISECT_STEPS = FLOAT_MIDPOINT_STEPS + 36
VMEM_LIMIT = 48 * 1024 * 1024

MXU_DTYPE = jnp.bfloat16

F32 = jnp.float32
I32 = jnp.int32


def _dot(a, b):
    return jnp.dot(a, b, preferred_element_type=F32)


def _sortable(bits):
    return bits ^ ((bits >> 31) & 0x7FFFFFFF)


_NEG_KEY = int(_sortable(np.float32(NEG_INF).view(np.int32)))


def _proj_kernel(x_ref, g_ref, wt_ref, ws_ref, gq_ref, gk_ref, cos_ref, sin_ref,
                 qt_ref, qit_ref, wit_ref, kk_ref, vt_ref, rest_ref, *, tm):
    x = x_ref[0]
    ms = jnp.mean(x * x, axis=-1, keepdims=True)
    hn = (x * lax.rsqrt(ms + EPS) * g_ref[...]).astype(MXU_DTYPE)

    for c in range(0, D_REST, 512):
        rest_ref[0, :, c:c + 512] = _dot(hn, ws_ref[:, c:c + 512]).astype(rest_ref.dtype)

    pt = lax.dot_general(wt_ref[...], hn, (((1,), (1,)), ((), ())), preferred_element_type=F32)
    cos = cos_ref[...]
    sin = sin_ref[...]
    half = HEAD_DIM // 2

    def rope(t):
        x1, x2 = t[:half], t[half:]
        return jnp.concatenate([x1 * cos - x2 * sin, x2 * cos + x1 * sin], axis=0)

    def norm(t, g):
        return t * lax.rsqrt(jnp.mean(t * t, axis=0, keepdims=True) + EPS) * g

    gq = gq_ref[...]
    q_scale = (HEAD_DIM ** -0.5) * math.log2(math.e)
    for h in range(N_HEADS):
        r0 = h * HEAD_DIM
        qh = rope(norm(pt[r0:r0 + HEAD_DIM], gq)) * q_scale
        qt_ref[0, r0:r0 + HEAD_DIM, :] = qh.astype(qt_ref.dtype)
        qih = rope(pt[D_A + 2 * HEAD_DIM + r0:D_A + 2 * HEAD_DIM + r0 + HEAD_DIM])
        qit_ref[0, r0:r0 + HEAD_DIM, :] = qih.astype(qit_ref.dtype)

    k_t = rope(norm(pt[D_A:D_A + HEAD_DIM], gk_ref[...]))
    v_t = pt[D_A + HEAD_DIM:D_A + 2 * HEAD_DIM]
    r_ki = 2 * D_A + 2 * HEAD_DIM
    ki_t = rope(pt[r_ki:r_ki + HEAD_DIM])
    wi_t = pt[r_ki + HEAD_DIM:r_ki + HEAD_DIM + N_HEADS]
    wit_ref[0] = wi_t * ((N_HEADS ** -0.5) * (HEAD_DIM ** -0.5))

    kk_t = jnp.concatenate([k_t, ki_t], axis=0)
    for c in range(tm // LANES):
        sl = slice(c * LANES, (c + 1) * LANES)
        kk_ref[0, sl, :] = kk_t[:, sl].T.astype(kk_ref.dtype)
        vt_ref[0, c] = v_t[:, sl].astype(vt_ref.dtype)


def _fold8(x, op):
    parts = [x[r:r + 8] for r in range(0, x.shape[0], 8)]
    while len(parts) > 1:
        nxt = [op(parts[a], parts[a + 1]) for a in range(0, len(parts) - 1, 2)]
        if len(parts) % 2:
            nxt.append(parts[-1])
        parts = nxt
    return parts[0]


def _loop_by_four(n, body, init):
    def quad(t, c):
        for u in range(4):
            c = body(4 * t + u, c)
        return c
    carry = lax.fori_loop(0, n // 4, quad, init)
    done = (n // 4) * 4
    carry = lax.cond(n - done >= 2, lambda c: body(done + 1, body(done, c)), lambda c: c, carry)
    return lax.cond(n % 2 == 1, lambda c: body(n - 1, c), lambda c: c, carry)


def _attn_kernel(qt_ref, qit_next_ref, wit_next_ref, qit_first_ref, wit_first_ref, kk_ref, vt_ref, o_ref,
                 keys2_scr, minmax2_scr, s_scr, acc_scr, *, k_top, n_rows, ks):
    i = pl.program_id(1)
    n_blocks = n_rows // QB
    nkb = ks // KB
    keys_scr = keys2_scr.at[i % 2]
    minmax_scr = minmax2_scr.at[i % 2]
    keys_next_scr = keys2_scr.at[(i + 1) % 2]
    minmax_next_scr = minmax2_scr.at[(i + 1) % 2]
    n_keys = n_rows - PAD
    lane = lax.broadcasted_iota(I32, (1, QB), 1)
    sub = lax.broadcasted_iota(I32, (KB, QB), 0)

    def block_geometry(t):
        q_row = t * QB + lane
        end = jnp.where(q_row < LANES, LANES, ((q_row - LANES) // CHUNK + 1) * CHUNK + LANES)
        return end, t // nkb + 1

    vis_end, ns = block_geometry(i)
    n_unseen = n_rows - ns * ks
    n_vis = vis_end - PAD

    zeros_half = jnp.zeros((HEAD_DIM, N_HEADS * QB), MXU_DTYPE)

    def stack_heads(ref):
        return jnp.concatenate([ref[0, h * HEAD_DIM:(h + 1) * HEAD_DIM, :] for h in range(N_HEADS)], axis=1)

    q_all = jnp.concatenate([stack_heads(qt_ref), zeros_half], axis=0)

    def score_step(j, carry, qi_all, w, end, dst_scr):
        kmin, kmax = carry
        base = pl.multiple_of(j * ks, ks)
        kk_sb = kk_ref[0, pl.ds(base, ks), :]
        s = _dot(kk_sb, qi_all)
        score = jnp.maximum(s[:, :QB], 0.0) * w[0:1, :]
        for h in range(1, N_HEADS):
            score = score + jnp.maximum(s[:, h * QB:(h + 1) * QB], 0.0) * w[h:h + 1, :]
        for r in range(nkb):
            off = base + r * KB
            vis = sub < (end - off)
            sc = jnp.where(vis, score[r * KB:(r + 1) * KB] + 0.0, NEG_INF)
            key = _sortable(lax.bitcast_convert_type(sc, I32))
            dst_scr[pl.ds(pl.multiple_of(off, KB), KB), :] = key
            kmax = jnp.maximum(kmax, _fold8(key, jnp.maximum))
            kmin = jnp.minimum(kmin, _fold8(jnp.where(vis, key, INT_MAX), jnp.minimum))
        return kmin, kmax

    no_minmax = (jnp.full((8, QB), INT_MAX, I32), jnp.full((8, QB), INT_MIN, I32))

    def indexer_operands(qit_ref, wit_ref):
        return jnp.concatenate([zeros_half, stack_heads(qit_ref)], axis=0), wit_ref[0]

    @pl.when(i == 0)
    def _():
        qi_all, w = indexer_operands(qit_first_ref, wit_first_ref)
        kmin, kmax = score_step(0, no_minmax, qi_all, w, vis_end, keys_scr)
        minmax_scr[0:8, :] = kmin
        minmax_scr[8:16, :] = kmax

    keys_scr[0:PAD, :] = jnp.full((PAD, QB), INT_MIN, I32)
    kmin = jnp.min(minmax_scr[0:8, :], axis=0, keepdims=True)
    kmax = jnp.max(minmax_scr[8:16, :], axis=0, keepdims=True)

    @pl.when(ns % 2 == 1)
    def _():
        keys_scr[pl.ds(pl.multiple_of(ns * ks, ks), ks), :] = jnp.full((ks, QB), INT_MIN, I32)

    def reduce_keys(term, op, init):
        def body(t, acc):
            base = pl.multiple_of(t * (2 * ks), 2 * ks)
            for r in range(2 * nkb):
                kb = keys_scr[pl.ds(pl.multiple_of(base + r * KB, KB), KB), :]
                acc = op(acc, _fold8(term(kb), op))
            return acc
        return lax.fori_loop(0, (ns + 1) // 2, body, jnp.full((8, QB), init, I32))

    def count_ge(cand):
        c = reduce_keys(lambda kb: jnp.where(kb >= cand, 1, 0), jnp.add, 0)
        return jnp.sum(c, axis=0, keepdims=True) + jnp.where(_NEG_KEY >= cand, n_unseen, 0)

    few_visible = n_vis < k_top
    lo0 = jnp.where(few_visible, jnp.minimum(kmin, _NEG_KEY), kmin)
    c_lo0 = n_vis + jnp.where(_NEG_KEY >= lo0, n_keys - n_vis, 0)
    top = jnp.maximum(kmax, _NEG_KEY)
    hi0 = jnp.where(top == INT_MAX, INT_MAX, top + 1)

    def finished(lo, hi, c_lo, c_hi):
        return (c_lo == k_top) | (lo + 1 >= hi) | (c_hi == k_top - 1)

    def bisect_cond(st):
        p, lo, hi, c_lo, c_hi = st
        pending = jnp.where(finished(lo, hi, c_lo, c_hi), 0.0, 1.0)
        return (p < MAX_BISECT_STEPS) & (jnp.max(pending) > 0.0)

    def bisect_body(st):
        for _ in range(BISECT_STEPS_PER_TEST):
            st = bisect_step(st)
        return st

    def bisect_step(st):
        p, lo, hi, c_lo, c_hi = st
        done = finished(lo, hi, c_lo, c_hi)
        lo_f = lax.bitcast_convert_type(_sortable(lo), F32)
        hi_f = lax.bitcast_convert_type(_sortable(hi), F32)
        cand_f = _sortable(lax.bitcast_convert_type(lo_f * 0.5 + hi_f * 0.5, I32))
        cand_k = (lo >> 1) + (hi >> 1) + (lo & hi & 1)
        use_f = (cand_f > lo) & (cand_f < hi) & (p < FLOAT_MIDPOINT_STEPS)
        cand = jnp.where(use_f, cand_f, cand_k)
        cand = jnp.where((p == 0) & (lo < 0) & (hi > 0), 0, cand)
        cand = jnp.where((p == 1) & (lo == 0) & (hi > 1) & (c_lo - k_top <= ZERO_TIE_MARGIN), 1, cand)
        cand = jnp.where(few_visible & (p == 0), _NEG_KEY + 1, cand)
        cand = jnp.minimum(jnp.maximum(cand, lo + 1), hi - 1)
        c = count_ge(cand)
        up = (c >= k_top) & jnp.logical_not(done)
        dn = (c < k_top) & jnp.logical_not(done)
        return (p + 1, jnp.where(up, cand, lo), jnp.where(dn, cand, hi),
                jnp.where(up, c, c_lo), jnp.where(dn, c, c_hi))

    st = (jnp.int32(0), lo0, hi0, c_lo0, jnp.zeros((1, QB), I32))
    untested = jnp.where(i * QB > LANES + k_top,
                         jnp.where(i * 3 > n_blocks, UNTESTED_BISECT_STEPS, UNTESTED_BISECT_STEPS - 4), 0)
    untested = untested // BISECT_STEPS_PER_TEST
    st = lax.fori_loop(0, untested, lambda _, s: bisect_body(s), st)
    _, lo, hi, c_lo, c_hi = lax.while_loop(bisect_cond, bisect_body, st)

    below_hi = reduce_keys(lambda kb: jnp.where(kb < hi, kb, INT_MIN), jnp.maximum, INT_MIN)
    below_hi = jnp.max(below_hi, axis=0, keepdims=True)
    below_hi = jnp.where((n_unseen > 0) & (_NEG_KEY < hi), jnp.maximum(below_hi, _NEG_KEY), below_hi)
    exact_cut = c_lo == k_top
    tau = jnp.where(exact_cut | (lo + 1 >= hi), lo, below_hi)

    n_tied_wanted = jnp.where(exact_cut, n_rows, k_top - c_hi).astype(F32)
    row_ge_col = (lax.broadcasted_iota(I32, (KB, KB), 0) >= lax.broadcasted_iota(I32, (KB, KB), 1))
    prefix_ones = jnp.where(row_ge_col, 1.0, 0.0).astype(MXU_DTYPE)

    def logits_body(j, carry):
        m, tied_before = carry
        base = pl.multiple_of(j * ks, ks)
        bias = []
        for r in range(nkb):
            off = base + r * KB
            kb = keys_scr[pl.ds(pl.multiple_of(off, KB), KB), :]
            tied = jnp.where(kb == tau, 1.0, 0.0).astype(MXU_DTYPE)
            rank = _dot(prefix_ones, tied)
            admissible = rank + tied_before <= n_tied_wanted
            tied_before = tied_before + rank[KB - 1:KB, :]
            thr = tau - jnp.where(admissible, 1, 0)
            thr = jnp.where(sub < vis_end - off, thr, INT_MAX)
            bias.append(jnp.where(kb > thr, 0.0, NEG_INF))
        bias = jnp.concatenate(bias, axis=0)
        s = _dot(kk_ref[0, pl.ds(base, ks), :], q_all) + jnp.concatenate([bias] * N_HEADS, axis=1)
        s_scr[pl.ds(base, ks), :] = s
        return jnp.maximum(m, jnp.max(s, axis=0, keepdims=True)), tied_before

    m, _ = _loop_by_four(ns, logits_body,
                         (jnp.full((1, N_HEADS * QB), NEG_INF, F32), jnp.zeros((1, QB), F32)))

    acc_scr[...] = jnp.zeros_like(acc_scr)
    ones_rows = jnp.ones((V_ROWS - HEAD_DIM, ks), MXU_DTYPE)

    def value_body(j, carry):
        base = pl.multiple_of(j * ks, ks)
        p = jnp.exp2(s_scr[pl.ds(base, ks), :] - m).astype(MXU_DTYPE)
        v_ext = jnp.concatenate(
            [jnp.concatenate([vt_ref[0, j * nkb + r] for r in range(nkb)], axis=1), ones_rows], axis=0)
        acc_scr[...] += _dot(v_ext, p)
        return carry

    @pl.when(i + 1 < n_blocks)
    def _():
        end_next, ns_next = block_geometry(i + 1)
        qi_all, w = indexer_operands(qit_next_ref, wit_next_ref)

        def both(j, carry):
            value_body(j, 0)
            return score_step(j, carry, qi_all, w, end_next, keys_next_scr)

        carry = _loop_by_four(ns, both, no_minmax)
        carry = lax.cond(ns_next > ns,
                         lambda c: score_step(ns, c, qi_all, w, end_next, keys_next_scr), lambda c: c, carry)
        minmax_next_scr[0:8, :] = carry[0]
        minmax_next_scr[8:16, :] = carry[1]

    @pl.when(i + 1 == n_blocks)
    def _():
        _loop_by_four(ns, value_body, 0)

    acc = acc_scr[...]
    out_t = acc[:HEAD_DIM] / acc[HEAD_DIM:HEAD_DIM + 1]
    for hp in range(N_HEADS // 2):
        blk = jnp.concatenate([out_t[:, (2 * hp) * QB:(2 * hp + 1) * QB],
                               out_t[:, (2 * hp + 1) * QB:(2 * hp + 2) * QB]], axis=0)
        o_ref[0, :, hp * LANES:(hp + 1) * LANES] = blk.T.astype(o_ref.dtype)


def _out_kernel(x_ref, attn_ref, rest_ref, halo_ref, band_ref, wa_ref, pw_ref, pb_ref, ps_ref, wb_ref,
                wo_ref, o_ref, *, tm):
    t = pl.program_id(1)
    row = t * tm + lax.broadcasted_iota(I32, (tm, LANES), 0)
    pos = row - PAD

    gate_a = rest_ref[0, :, 0:D_A].astype(F32)
    a = attn_ref[0].astype(F32) * (gate_a * jax.nn.sigmoid(gate_a))
    y_a = _dot(a.astype(MXU_DTYPE), wa_ref[...])

    u = rest_ref[0, :, D_A:D_A + D_B]
    halo = jnp.where(t > 0, halo_ref[0], jnp.zeros_like(halo_ref[0]))
    u_ext = jnp.concatenate([halo, u], axis=0)
    n_chunks = tm // LANES
    pooled = []
    for g, win in enumerate(POOL_WINDOWS):
        cs = slice(g * POOL_GROUP, (g + 1) * POOL_GROUP)
        windows = jnp.concatenate([u_ext[c * LANES:(c + 2) * LANES, cs] for c in range(n_chunks)], axis=1)
        sums = _dot(band_ref[g], windows)
        window_sum = jnp.concatenate([sums[:, c * LANES:(c + 1) * LANES] for c in range(n_chunks)], axis=0)
        cnt = jnp.clip(pos + 1, 1, win).astype(F32)
        pooled.append((window_sum / cnt - u[:, cs].astype(F32)).astype(MXU_DTYPE))
    mixed = _dot(jnp.concatenate(pooled, axis=1), pw_ref[...])
    mixed = (mixed + pb_ref[...]) * ps_ref[...]
    gate_b = rest_ref[0, :, D_A + D_B:D_A + 2 * D_B].astype(F32)
    z_b = mixed * (gate_b * jax.nn.sigmoid(gate_b))
    y_b = _dot(z_b.astype(MXU_DTYPE), wb_ref[...])

    c0 = D_A + 2 * D_B
    g_a = jax.nn.sigmoid(rest_ref[0, :, c0:c0 + D_MODEL].astype(F32))
    g_b = jax.nn.sigmoid(rest_ref[0, :, c0 + D_MODEL:c0 + 2 * D_MODEL].astype(F32))
    mix = g_a * y_a + g_b * y_b
    out = x_ref[0] + _dot(mix.astype(MXU_DTYPE), wo_ref[...])
    row_full = t * tm + lax.broadcasted_iota(I32, (tm, D_MODEL), 0)
    o_ref[0] = jnp.where(row_full >= PAD, out, 0.0)


def _row_tile(n_rows):
    for tm in (384, 256, 128):
        if n_rows % tm == 0:
            return tm
    raise ValueError(f"sequence rows {n_rows} must be a multiple of 128")


def _key_step(n_rows):
    return 3 * KB if n_rows % (3 * KB) == 0 else KB


def _params(sem):
    return pltpu.CompilerParams(dimension_semantics=sem, vmem_limit_bytes=VMEM_LIMIT)


def _const_spec(shape):
    return pl.BlockSpec(shape, lambda b, t: (0,) * len(shape))


def _layer(h, lw, cos_t, sin_t, band, *, k_top):
    B, n_rows, _ = h.shape
    tm = _row_tile(n_rows)
    nt = n_rows // tm
    nblk = n_rows // LANES
    act = MXU_DTYPE

    qt, qit, wit, kk, vt, rest = pl.pallas_call(
        functools.partial(_proj_kernel, tm=tm),
        grid=(B, nt),
        in_specs=[
            pl.BlockSpec((1, tm, D_MODEL), lambda b, t: (b, t, 0)),
            _const_spec((1, D_MODEL)),
            _const_spec((N_T_ROWS, D_MODEL)),
            _const_spec((D_MODEL, D_REST)),
            _const_spec((HEAD_DIM, tm)),
            _const_spec((HEAD_DIM, tm)),
            pl.BlockSpec((HEAD_DIM // 2, tm), lambda b, t: (0, t)),
            pl.BlockSpec((HEAD_DIM // 2, tm), lambda b, t: (0, t)),
        ],
        out_specs=[
            pl.BlockSpec((1, D_A, tm), lambda b, t: (b, 0, t)),
            pl.BlockSpec((1, D_A, tm), lambda b, t: (b, 0, t)),
            pl.BlockSpec((1, N_HEADS, tm), lambda b, t: (b, 0, t)),
            pl.BlockSpec((1, tm, LANES), lambda b, t: (b, t, 0)),
            pl.BlockSpec((1, tm // LANES, HEAD_DIM, LANES), lambda b, t: (b, t, 0, 0)),
            pl.BlockSpec((1, tm, D_REST), lambda b, t: (b, t, 0)),
        ],
        out_shape=[
            jax.ShapeDtypeStruct((B, D_A, n_rows), act),
            jax.ShapeDtypeStruct((B, D_A, n_rows), act),
            jax.ShapeDtypeStruct((B, N_HEADS, n_rows), F32),
            jax.ShapeDtypeStruct((B, n_rows, LANES), act),
            jax.ShapeDtypeStruct((B, nblk, HEAD_DIM, LANES), act),
            jax.ShapeDtypeStruct((B, n_rows, D_REST), act),
        ],
        compiler_params=_params(("parallel", "parallel")),
        name="proj",
    )(h, lw["g"], lw["wt"], lw["ws"], lw["gq"], lw["gk"], cos_t, sin_t)

    assert n_rows - PAD >= k_top
    ks = _key_step(n_rows)
    attn = pl.pallas_call(
        functools.partial(_attn_kernel, k_top=k_top, n_rows=n_rows, ks=ks),
        grid=(B, nblk),
        in_specs=[
            pl.BlockSpec((1, D_A, QB), lambda b, i: (b, 0, i)),
            pl.BlockSpec((1, D_A, QB), lambda b, i: (b, 0, jnp.minimum(i + 1, nblk - 1))),
            pl.BlockSpec((1, N_HEADS, QB), lambda b, i: (b, 0, jnp.minimum(i + 1, nblk - 1))),
            pl.BlockSpec((1, D_A, QB), lambda b, i: (b, 0, 0)),
            pl.BlockSpec((1, N_HEADS, QB), lambda b, i: (b, 0, 0)),
            pl.BlockSpec((1, n_rows, LANES), lambda b, i: (b, 0, 0)),
            pl.BlockSpec((1, nblk, HEAD_DIM, LANES), lambda b, i: (b, 0, 0, 0)),
        ],
        out_specs=pl.BlockSpec((1, QB, D_A), lambda b, i: (b, i, 0)),
        out_shape=jax.ShapeDtypeStruct((B, n_rows, D_A), act),
        scratch_shapes=[
            pltpu.VMEM((2, n_rows + ks, QB), I32),
            pltpu.VMEM((2, 16, QB), I32),
            pltpu.VMEM((n_rows, N_HEADS * QB), F32),
            pltpu.VMEM((V_ROWS, N_HEADS * QB), F32),
        ],
        compiler_params=_params(("arbitrary", "arbitrary")),
        name="attn",
    )(qt, qit, wit, qit, wit, kk, vt)

    halo_blocks = tm // LANES
    out = pl.pallas_call(
        functools.partial(_out_kernel, tm=tm),
        grid=(B, nt),
        in_specs=[
            pl.BlockSpec((1, tm, D_MODEL), lambda b, t: (b, t, 0)),
            pl.BlockSpec((1, tm, D_A), lambda b, t: (b, t, 0)),
            pl.BlockSpec((1, tm, D_REST), lambda b, t: (b, t, 0)),
            pl.BlockSpec((1, LANES, D_B), lambda b, t: (b, jnp.maximum(t * halo_blocks - 1, 0), 1)),
            _const_spec((len(POOL_WINDOWS), LANES, 2 * LANES)),
            _const_spec((D_A, D_MODEL)),
            _const_spec((D_B, D_B)),
            _const_spec((1, D_B)),
            _const_spec((1, D_B)),
            _const_spec((D_B, D_MODEL)),
            _const_spec((D_MODEL, D_MODEL)),
        ],
        out_specs=pl.BlockSpec((1, tm, D_MODEL), lambda b, t: (b, t, 0)),
        out_shape=jax.ShapeDtypeStruct((B, n_rows, D_MODEL), F32),
        compiler_params=_params(("parallel", "parallel")),
        name="outproj",
    )(h, attn, rest, rest, band, lw["wa"], lw["pw"], lw["pb"], lw["ps"], lw["wb"], lw["wo"])
    return out


def _prep_layer_weights(l, tm, norm_gain, w_in, q_norm_gain, k_norm_gain, pool_w, pool_b, pool_scale,
                        w_branch_a, w_branch_b, w_out):
    w = w_in[l]
    o_q, o_ga = 0, D_A + 2 * HEAD_DIM
    o_ub = o_ga + D_A
    o_gb = o_ub + D_B
    o_qi = o_gb + D_B
    o_ki = o_qi + D_A
    o_wi = o_ki + HEAD_DIM
    o_mg = o_wi + N_HEADS
    w_t = jnp.concatenate([w[:, o_q:o_ga], w[:, o_qi:o_mg],
                           jnp.zeros((D_MODEL, N_T_ROWS - (o_ga - o_q) - (o_mg - o_qi)), w.dtype)], axis=1).T
    w_s = jnp.concatenate([w[:, o_ga:o_qi], w[:, o_mg:]], axis=1)
    return {
        "g": norm_gain[l].reshape(1, D_MODEL).astype(F32),
        "wt": w_t.astype(MXU_DTYPE),
        "ws": w_s.astype(MXU_DTYPE),
        "gq": jnp.broadcast_to(q_norm_gain[l].astype(F32)[:, None], (HEAD_DIM, tm)),
        "gk": jnp.broadcast_to(k_norm_gain[l].astype(F32)[:, None], (HEAD_DIM, tm)),
        "wa": w_branch_a[l].astype(MXU_DTYPE),
        "pw": jax.scipy.linalg.block_diag(*[pool_w[l, g] for g in range(len(POOL_WINDOWS))]).astype(MXU_DTYPE),
        "pb": pool_b[l].reshape(1, D_B).astype(F32),
        "ps": pool_scale[l].reshape(1, D_B).astype(F32),
        "wb": w_branch_b[l].astype(MXU_DTYPE),
        "wo": w_out[l].astype(MXU_DTYPE),
    }


@jax.jit
def kernel(x, meta_tokens, norm_gain, w_in, q_norm_gain, k_norm_gain, pool_w, pool_b, pool_scale,
           w_branch_a, w_branch_b, w_out):
    B, S, _ = x.shape
    assert S % LANES == 0, "sequence length must be a multiple of 128"
    k_top = min(TOPK_MAX, S // 4)
    n_rows = LANES + S
    tm = _row_tile(n_rows)

    meta = jnp.broadcast_to(meta_tokens.astype(x.dtype)[None], (B, N_META, D_MODEL))
    h = jnp.concatenate([jnp.zeros((B, PAD, D_MODEL), x.dtype), meta, x], axis=1)

    pos = jnp.maximum(jnp.arange(n_rows, dtype=F32) - PAD, 0.0)
    inv_freq = 1.0 / (ROPE_THETA ** (jnp.arange(0, HEAD_DIM, 2, dtype=F32) / HEAD_DIM))
    ang = inv_freq[:, None] * pos[None, :]
    cos_t, sin_t = jnp.cos(ang), jnp.sin(ang)

    r = jnp.arange(LANES)[:, None]
    c = jnp.arange(2 * LANES)[None, :] - LANES
    band = jnp.stack([((c <= r) & (c > r - w)) for w in POOL_WINDOWS]).astype(MXU_DTYPE)

    for l in range(DEPTH):
        lw = _prep_layer_weights(l, tm, norm_gain, w_in, q_norm_gain, k_norm_gain, pool_w, pool_b, pool_scale,
                                 w_branch_a, w_branch_b, w_out)
        h = _layer(h, lw, cos_t, sin_t, band, k_top=k_top)
    return h[:, LANES:]
```

```python
import functools
import math

import jax
import jax.numpy as jnp
import numpy as np
from jax import lax
from jax.experimental import pallas as pl
from jax.experimental.pallas import tpu as pltpu

D_MODEL = 1024
DEPTH = 4
CHUNK = 64
N_META = 16
N_HEADS = 8
HEAD_DIM = 64
D_A = N_HEADS * HEAD_DIM
D_B = 512
POOL_WINDOWS = (2, 4, 8, 16)
POOL_GROUP = 128
TOPK_MAX = 256
ROPE_THETA = 10000.0
EPS = 1e-6
NEG_INF = -1e30

LANES = 128
PAD = LANES - N_META
KB = LANES
QB = LANES
V_ROWS = HEAD_DIM + 16
N_T_ROWS = 2 * D_A + 3 * HEAD_DIM + 16
D_REST = D_A + D_B + D_B + 2 * D_MODEL
INT_MIN = -2 ** 31
INT_MAX = 2 ** 31 - 1
FLOAT_MIDPOINT_STEPS = 24
BISECT_STEPS_PER_TEST = 2
UNTESTED_BISECT_STEPS = 14
ZERO_TIE_MARGIN = 128
MAX_BISECT_STEPS = FLOAT_MIDPOINT_STEPS + 36
VMEM_LIMIT = 48 * 1024 * 1024

MXU_DTYPE = jnp.bfloat16

F32 = jnp.float32
I32 = jnp.int32


def _dot(a, b):
    return jnp.dot(a, b, preferred_element_type=F32)


def _sortable(bits):
    return bits ^ ((bits >> 31) & 0x7FFFFFFF)


_NEG_KEY = int(_sortable(np.float32(NEG_INF).view(np.int32)))


def _proj_kernel(x_ref, g_ref, wt_ref, ws_ref, gq_ref, gk_ref, cos_ref, sin_ref,
                 qt_ref, qit_ref, wit_ref, kk_ref, vt_ref, rest_ref, *, tm):
    x = x_ref[0]
    ms = jnp.mean(x * x, axis=-1, keepdims=True)
    hn = (x * lax.rsqrt(ms + EPS) * g_ref[...]).astype(MXU_DTYPE)

    for c in range(0, D_REST, 512):
        rest_ref[0, :, c:c + 512] = _dot(hn, ws_ref[:, c:c + 512]).astype(rest_ref.dtype)

    pt = lax.dot_general(wt_ref[...], hn, (((1,), (1,)), ((), ())), preferred_element_type=F32)
    cos = cos_ref[...]
    sin = sin_ref[...]
    half = HEAD_DIM // 2

    def rope(t):
        x1, x2 = t[:half], t[half:]
        return jnp.concatenate([x1 * cos - x2 * sin, x2 * cos + x1 * sin], axis=0)

    def norm(t, g):
        return t * lax.rsqrt(jnp.mean(t * t, axis=0, keepdims=True) + EPS) * g

    gq = gq_ref[...]
    q_scale = (HEAD_DIM ** -0.5) * math.log2(math.e)
    for h in range(N_HEADS):
        r0 = h * HEAD_DIM
        qh = rope(norm(pt[r0:r0 + HEAD_DIM], gq)) * q_scale
        qt_ref[0, r0:r0 + HEAD_DIM, :] = qh.astype(qt_ref.dtype)
        qih = rope(pt[D_A + 2 * HEAD_DIM + r0:D_A + 2 * HEAD_DIM + r0 + HEAD_DIM])
        qit_ref[0, r0:r0 + HEAD_DIM, :] = qih.astype(qit_ref.dtype)

    k_t = rope(norm(pt[D_A:D_A + HEAD_DIM], gk_ref[...]))
    v_t = pt[D_A + HEAD_DIM:D_A + 2 * HEAD_DIM]
    r_ki = 2 * D_A + 2 * HEAD_DIM
    ki_t = rope(pt[r_ki:r_ki + HEAD_DIM])
    wi_t = pt[r_ki + HEAD_DIM:r_ki + HEAD_DIM + N_HEADS]
    wit_ref[0] = wi_t * ((N_HEADS ** -0.5) * (HEAD_DIM ** -0.5))

    kk_t = jnp.concatenate([k_t, ki_t], axis=0)
    for c in range(tm // LANES):
        sl = slice(c * LANES, (c + 1) * LANES)
        kk_ref[0, sl, :] = kk_t[:, sl].T.astype(kk_ref.dtype)
        vt_ref[0, c] = v_t[:, sl].astype(vt_ref.dtype)


def _fold8(x, op):
    parts = [x[r:r + 8] for r in range(0, x.shape[0], 8)]
    while len(parts) > 1:
        nxt = [op(parts[a], parts[a + 1]) for a in range(0, len(parts) - 1, 2)]
        if len(parts) % 2:
            nxt.append(parts[-1])
        parts = nxt
    return parts[0]


def _loop_by_four(n, body, init):
    def quad(t, c):
        for u in range(4):
            c = body(4 * t + u, c)
        return c
    carry = lax.fori_loop(0, n // 4, quad, init)
    done = (n // 4) * 4
    carry = lax.cond(n - done >= 2, lambda c: body(done + 1, body(done, c)), lambda c: c, carry)
    return lax.cond(n % 2 == 1, lambda c: body(n - 1, c), lambda c: c, carry)


def _attn_kernel(qt_ref, qit_next_ref, wit_next_ref, qit_first_ref, wit_first_ref, kk_ref, vt_ref, o_ref,
                 keys2_scr, minmax2_scr, s_scr, acc_scr, *, k_top, n_rows, ks):
    i = pl.program_id(1)
    n_blocks = n_rows // QB
    nkb = ks // KB
    keys_scr = keys2_scr.at[i % 2]
    minmax_scr = minmax2_scr.at[i % 2]
    keys_next_scr = keys2_scr.at[(i + 1) % 2]
    minmax_next_scr = minmax2_scr.at[(i + 1) % 2]
    n_keys = n_rows - PAD
    lane = lax.broadcasted_iota(I32, (1, QB), 1)
    sub = lax.broadcasted_iota(I32, (KB, QB), 0)

    def block_geometry(t):
        q_row = t * QB + lane
        end = jnp.where(q_row < LANES, LANES, ((q_row - LANES) // CHUNK + 1) * CHUNK + LANES)
        return end, t // nkb + 1

    vis_end, ns = block_geometry(i)
    n_unseen = n_rows - ns * ks
    n_vis = vis_end - PAD

    zeros_half = jnp.zeros((HEAD_DIM, N_HEADS * QB), MXU_DTYPE)

    def stack_heads(ref):
        return jnp.concatenate([ref[0, h * HEAD_DIM:(h + 1) * HEAD_DIM, :] for h in range(N_HEADS)], axis=1)

    q_all = jnp.concatenate([stack_heads(qt_ref), zeros_half], axis=0)

    def score_step(j, carry, qi_all, w, end, dst_scr):
        kmin, kmax = carry
        base = pl.multiple_of(j * ks, ks)
        kk_sb = kk_ref[0, pl.ds(base, ks), :]
        s = _dot(kk_sb, qi_all)
        score = jnp.maximum(s[:, :QB], 0.0) * w[0:1, :]
        for h in range(1, N_HEADS):
            score = score + jnp.maximum(s[:, h * QB:(h + 1) * QB], 0.0) * w[h:h + 1, :]
        for r in range(nkb):
            off = base + r * KB
            vis = sub < (end - off)
            sc = jnp.where(vis, score[r * KB:(r + 1) * KB] + 0.0, NEG_INF)
            key = _sortable(lax.bitcast_convert_type(sc, I32))
            dst_scr[pl.ds(pl.multiple_of(off, KB), KB), :] = key
            kmax = jnp.maximum(kmax, _fold8(key, jnp.maximum))
            kmin = jnp.minimum(kmin, _fold8(jnp.where(vis, key, INT_MAX), jnp.minimum))
        return kmin, kmax

    no_minmax = (jnp.full((8, QB), INT_MAX, I32), jnp.full((8, QB), INT_MIN, I32))

    def indexer_operands(qit_ref, wit_ref):
        return jnp.concatenate([zeros_half, stack_heads(qit_ref)], axis=0), wit_ref[0]

    @pl.when(i == 0)
    def _():
        qi_all, w = indexer_operands(qit_first_ref, wit_first_ref)
        kmin, kmax = score_step(0, no_minmax, qi_all, w, vis_end, keys_scr)
        minmax_scr[0:8, :] = kmin
        minmax_scr[8:16, :] = kmax

    keys_scr[0:PAD, :] = jnp.full((PAD, QB), INT_MIN, I32)
    kmin = jnp.min(minmax_scr[0:8, :], axis=0, keepdims=True)
    kmax = jnp.max(minmax_scr[8:16, :], axis=0, keepdims=True)

    @pl.when(ns % 2 == 1)
    def _():
        keys_scr[pl.ds(pl.multiple_of(ns * ks, ks), ks), :] = jnp.full((ks, QB), INT_MIN, I32)

    def reduce_keys(term, op, init):
        def body(t, acc):
            base = pl.multiple_of(t * (2 * ks), 2 * ks)
            for r in range(2 * nkb):
                kb = keys_scr[pl.ds(pl.multiple_of(base + r * KB, KB), KB), :]
                acc = op(acc, _fold8(term(kb), op))
            return acc
        return lax.fori_loop(0, (ns + 1) // 2, body, jnp.full((8, QB), init, I32))

    def count_ge(cand):
        c = reduce_keys(lambda kb: jnp.where(kb >= cand, 1, 0), jnp.add, 0)
        return jnp.sum(c, axis=0, keepdims=True) + jnp.where(_NEG_KEY >= cand, n_unseen, 0)

    few_visible = n_vis < k_top
    lo0 = jnp.where(few_visible, jnp.minimum(kmin, _NEG_KEY), kmin)
    c_lo0 = n_vis + jnp.where(_NEG_KEY >= lo0, n_keys - n_vis, 0)
    top = jnp.maximum(kmax, _NEG_KEY)
    hi0 = jnp.where(top == INT_MAX, INT_MAX, top + 1)

    def finished(lo, hi, c_lo, c_hi):
        return (c_lo == k_top) | (lo + 1 >= hi) | (c_hi == k_top - 1)

    def bisect_cond(st):
        p, lo, hi, c_lo, c_hi = st
        pending = jnp.where(finished(lo, hi, c_lo, c_hi), 0.0, 1.0)
        return (p < MAX_BISECT_STEPS) & (jnp.max(pending) > 0.0)

    def bisect_body(st):
        for _ in range(BISECT_STEPS_PER_TEST):
            st = bisect_step(st)
        return st

    def bisect_step(st):
        p, lo, hi, c_lo, c_hi = st
        done = finished(lo, hi, c_lo, c_hi)
        lo_f = lax.bitcast_convert_type(_sortable(lo), F32)
        hi_f = lax.bitcast_convert_type(_sortable(hi), F32)
        cand_f = _sortable(lax.bitcast_convert_type(lo_f * 0.5 + hi_f * 0.5, I32))
        cand_k = (lo >> 1) + (hi >> 1) + (lo & hi & 1)
        use_f = (cand_f > lo) & (cand_f < hi) & (p < FLOAT_MIDPOINT_STEPS)
        cand = jnp.where(use_f, cand_f, cand_k)
        cand = jnp.where((p == 0) & (lo < 0) & (hi > 0), 0, cand)
        cand = jnp.where((p == 1) & (lo == 0) & (hi > 1) & (c_lo - k_top <= ZERO_TIE_MARGIN), 1, cand)
        cand = jnp.where(few_visible & (p == 0), _NEG_KEY + 1, cand)
        cand = jnp.minimum(jnp.maximum(cand, lo + 1), hi - 1)
        c = count_ge(cand)
        up = (c >= k_top) & jnp.logical_not(done)
        dn = (c < k_top) & jnp.logical_not(done)
        return (p + 1, jnp.where(up, cand, lo), jnp.where(dn, cand, hi),
                jnp.where(up, c, c_lo), jnp.where(dn, c, c_hi))

    st = (jnp.int32(0), lo0, hi0, c_lo0, jnp.zeros((1, QB), I32))
    untested = jnp.where(i * QB > LANES + k_top,
                         jnp.where(i * 3 > n_blocks, UNTESTED_BISECT_STEPS, UNTESTED_BISECT_STEPS - 2), 0)
    untested = untested // BISECT_STEPS_PER_TEST
    st = lax.fori_loop(0, untested, lambda _, s: bisect_body(s), st)
    _, lo, hi, c_lo, c_hi = lax.while_loop(bisect_cond, bisect_body, st)

    below_hi = reduce_keys(lambda kb: jnp.where(kb < hi, kb, INT_MIN), jnp.maximum, INT_MIN)
    below_hi = jnp.max(below_hi, axis=0, keepdims=True)
    below_hi = jnp.where((n_unseen > 0) & (_NEG_KEY < hi), jnp.maximum(below_hi, _NEG_KEY), below_hi)
    exact_cut = c_lo == k_top
    tau = jnp.where(exact_cut | (lo + 1 >= hi), lo, below_hi)

    n_tied_wanted = jnp.where(exact_cut, n_rows, k_top - c_hi).astype(F32)
    row_ge_col = (lax.broadcasted_iota(I32, (KB, KB), 0) >= lax.broadcasted_iota(I32, (KB, KB), 1))
    prefix_ones = jnp.where(row_ge_col, 1.0, 0.0).astype(MXU_DTYPE)

    def logits_body(j, carry):
        m, tied_before = carry
        base = pl.multiple_of(j * ks, ks)
        bias = []
        for r in range(nkb):
            off = base + r * KB
            kb = keys_scr[pl.ds(pl.multiple_of(off, KB), KB), :]
            tied = jnp.where(kb == tau, 1.0, 0.0).astype(MXU_DTYPE)
            rank = _dot(prefix_ones, tied)
            admissible = rank + tied_before <= n_tied_wanted
            tied_before = tied_before + rank[KB - 1:KB, :]
            thr = tau - jnp.where(admissible, 1, 0)
            thr = jnp.where(sub < vis_end - off, thr, INT_MAX)
            bias.append(jnp.where(kb > thr, 0.0, NEG_INF))
        bias = jnp.concatenate(bias, axis=0)
        s = _dot(kk_ref[0, pl.ds(base, ks), :], q_all) + jnp.concatenate([bias] * N_HEADS, axis=1)
        s_scr[pl.ds(base, ks), :] = s
        return jnp.maximum(m, jnp.max(s, axis=0, keepdims=True)), tied_before

    m, _ = _loop_by_four(ns, logits_body,
                         (jnp.full((1, N_HEADS * QB), NEG_INF, F32), jnp.zeros((1, QB), F32)))

    acc_scr[...] = jnp.zeros_like(acc_scr)
    ones_rows = jnp.ones((V_ROWS - HEAD_DIM, ks), MXU_DTYPE)

    def value_body(j, carry):
        base = pl.multiple_of(j * ks, ks)
        p = jnp.exp2(s_scr[pl.ds(base, ks), :] - m).astype(MXU_DTYPE)
        v_ext = jnp.concatenate(
            [jnp.concatenate([vt_ref[0, j * nkb + r] for r in range(nkb)], axis=1), ones_rows], axis=0)
        acc_scr[...] += _dot(v_ext, p)
        return carry

    @pl.when(i + 1 < n_blocks)
    def _():
        end_next, ns_next = block_geometry(i + 1)
        qi_all, w = indexer_operands(qit_next_ref, wit_next_ref)

        def both(j, carry):
            value_body(j, 0)
            return score_step(j, carry, qi_all, w, end_next, keys_next_scr)

        carry = _loop_by_four(ns, both, no_minmax)
        carry = lax.cond(ns_next > ns,
                         lambda c: score_step(ns, c, qi_all, w, end_next, keys_next_scr), lambda c: c, carry)
        minmax_next_scr[0:8, :] = carry[0]
        minmax_next_scr[8:16, :] = carry[1]

    @pl.when(i + 1 == n_blocks)
    def _():
        _loop_by_four(ns, value_body, 0)

    acc = acc_scr[...]
    out_t = acc[:HEAD_DIM] / acc[HEAD_DIM:HEAD_DIM + 1]
    for hp in range(N_HEADS // 2):
        blk = jnp.concatenate([out_t[:, (2 * hp) * QB:(2 * hp + 1) * QB],
                               out_t[:, (2 * hp + 1) * QB:(2 * hp + 2) * QB]], axis=0)
        o_ref[0, :, hp * LANES:(hp + 1) * LANES] = blk.T.astype(o_ref.dtype)


def _out_kernel(x_ref, attn_ref, rest_ref, halo_ref, band_ref, wa_ref, pw_ref, pb_ref, ps_ref, wb_ref,
                wo_ref, o_ref, *, tm):
    t = pl.program_id(1)
    row = t * tm + lax.broadcasted_iota(I32, (tm, LANES), 0)
    pos = row - PAD

    gate_a = rest_ref[0, :, 0:D_A].astype(F32)
    a = attn_ref[0].astype(F32) * (gate_a * jax.nn.sigmoid(gate_a))
    y_a = _dot(a.astype(MXU_DTYPE), wa_ref[...])

    u = rest_ref[0, :, D_A:D_A + D_B]
    halo = jnp.where(t > 0, halo_ref[0], jnp.zeros_like(halo_ref[0]))
    u_ext = jnp.concatenate([halo, u], axis=0)
    n_chunks = tm // LANES
    pooled = []
    for g, win in enumerate(POOL_WINDOWS):
        cs = slice(g * POOL_GROUP, (g + 1) * POOL_GROUP)
        windows = jnp.concatenate([u_ext[c * LANES:(c + 2) * LANES, cs] for c in range(n_chunks)], axis=1)
        sums = _dot(band_ref[g], windows)
        window_sum = jnp.concatenate([sums[:, c * LANES:(c + 1) * LANES] for c in range(n_chunks)], axis=0)
        cnt = jnp.clip(pos + 1, 1, win).astype(F32)
        pooled.append((window_sum / cnt - u[:, cs].astype(F32)).astype(MXU_DTYPE))
    mixed = _dot(jnp.concatenate(pooled, axis=1), pw_ref[...])
    mixed = (mixed + pb_ref[...]) * ps_ref[...]
    gate_b = rest_ref[0, :, D_A + D_B:D_A + 2 * D_B].astype(F32)
    z_b = mixed * (gate_b * jax.nn.sigmoid(gate_b))
    y_b = _dot(z_b.astype(MXU_DTYPE), wb_ref[...])

    c0 = D_A + 2 * D_B
    g_a = jax.nn.sigmoid(rest_ref[0, :, c0:c0 + D_MODEL].astype(F32))
    g_b = jax.nn.sigmoid(rest_ref[0, :, c0 + D_MODEL:c0 + 2 * D_MODEL].astype(F32))
    mix = g_a * y_a + g_b * y_b
    out = x_ref[0] + _dot(mix.astype(MXU_DTYPE), wo_ref[...])
    row_full = t * tm + lax.broadcasted_iota(I32, (tm, D_MODEL), 0)
    o_ref[0] = jnp.where(row_full >= PAD, out, 0.0)


def _row_tile(n_rows):
    for tm in (384, 256, 128):
        if n_rows % tm == 0:
            return tm
    raise ValueError(f"sequence rows {n_rows} must be a multiple of 128")


def _key_step(n_rows):
    return 3 * KB if n_rows % (3 * KB) == 0 else KB


def _params(sem):
    return pltpu.CompilerParams(dimension_semantics=sem, vmem_limit_bytes=VMEM_LIMIT)


def _const_spec(shape):
    return pl.BlockSpec(shape, lambda b, t: (0,) * len(shape))


def _layer(h, lw, cos_t, sin_t, band, *, k_top):
    B, n_rows, _ = h.shape
    tm = _row_tile(n_rows)
    nt = n_rows // tm
    nblk = n_rows // LANES
    act = MXU_DTYPE

    qt, qit, wit, kk, vt, rest = pl.pallas_call(
        functools.partial(_proj_kernel, tm=tm),
        grid=(B, nt),
        in_specs=[
            pl.BlockSpec((1, tm, D_MODEL), lambda b, t: (b, t, 0)),
            _const_spec((1, D_MODEL)),
            _const_spec((N_T_ROWS, D_MODEL)),
            _const_spec((D_MODEL, D_REST)),
            _const_spec((HEAD_DIM, tm)),
            _const_spec((HEAD_DIM, tm)),
            pl.BlockSpec((HEAD_DIM // 2, tm), lambda b, t: (0, t)),
            pl.BlockSpec((HEAD_DIM // 2, tm), lambda b, t: (0, t)),
        ],
        out_specs=[
            pl.BlockSpec((1, D_A, tm), lambda b, t: (b, 0, t)),
            pl.BlockSpec((1, D_A, tm), lambda b, t: (b, 0, t)),
            pl.BlockSpec((1, N_HEADS, tm), lambda b, t: (b, 0, t)),
            pl.BlockSpec((1, tm, LANES), lambda b, t: (b, t, 0)),
            pl.BlockSpec((1, tm // LANES, HEAD_DIM, LANES), lambda b, t: (b, t, 0, 0)),
            pl.BlockSpec((1, tm, D_REST), lambda b, t: (b, t, 0)),
        ],
        out_shape=[
            jax.ShapeDtypeStruct((B, D_A, n_rows), act),
            jax.ShapeDtypeStruct((B, D_A, n_rows), act),
            jax.ShapeDtypeStruct((B, N_HEADS, n_rows), F32),
            jax.ShapeDtypeStruct((B, n_rows, LANES), act),
            jax.ShapeDtypeStruct((B, nblk, HEAD_DIM, LANES), act),
            jax.ShapeDtypeStruct((B, n_rows, D_REST), act),
        ],
        compiler_params=_params(("parallel", "parallel")),
        name="proj",
    )(h, lw["g"], lw["wt"], lw["ws"], lw["gq"], lw["gk"], cos_t, sin_t)

    assert n_rows - PAD >= k_top
    ks = _key_step(n_rows)
    attn = pl.pallas_call(
        functools.partial(_attn_kernel, k_top=k_top, n_rows=n_rows, ks=ks),
        grid=(B, nblk),
        in_specs=[
            pl.BlockSpec((1, D_A, QB), lambda b, i: (b, 0, i)),
            pl.BlockSpec((1, D_A, QB), lambda b, i: (b, 0, jnp.minimum(i + 1, nblk - 1))),
            pl.BlockSpec((1, N_HEADS, QB), lambda b, i: (b, 0, jnp.minimum(i + 1, nblk - 1))),
            pl.BlockSpec((1, D_A, QB), lambda b, i: (b, 0, 0)),
            pl.BlockSpec((1, N_HEADS, QB), lambda b, i: (b, 0, 0)),
            pl.BlockSpec((1, n_rows, LANES), lambda b, i: (b, 0, 0)),
            pl.BlockSpec((1, nblk, HEAD_DIM, LANES), lambda b, i: (b, 0, 0, 0)),
        ],
        out_specs=pl.BlockSpec((1, QB, D_A), lambda b, i: (b, i, 0)),
        out_shape=jax.ShapeDtypeStruct((B, n_rows, D_A), act),
        scratch_shapes=[
            pltpu.VMEM((2, n_rows + ks, QB), I32),
            pltpu.VMEM((2, 16, QB), I32),
            pltpu.VMEM((n_rows, N_HEADS * QB), F32),
            pltpu.VMEM((V_ROWS, N_HEADS * QB), F32),
        ],
        compiler_params=_params(("arbitrary", "arbitrary")),
        name="attn",
    )(qt, qit, wit, qit, wit, kk, vt)

    halo_blocks = tm // LANES
    out = pl.pallas_call(
        functools.partial(_out_kernel, tm=tm),
        grid=(B, nt),
        in_specs=[
            pl.BlockSpec((1, tm, D_MODEL), lambda b, t: (b, t, 0)),
            pl.BlockSpec((1, tm, D_A), lambda b, t: (b, t, 0)),
            pl.BlockSpec((1, tm, D_REST), lambda b, t: (b, t, 0)),
            pl.BlockSpec((1, LANES, D_B), lambda b, t: (b, jnp.maximum(t * halo_blocks - 1, 0), 1)),
            _const_spec((len(POOL_WINDOWS), LANES, 2 * LANES)),
            _const_spec((D_A, D_MODEL)),
            _const_spec((D_B, D_B)),
            _const_spec((1, D_B)),
            _const_spec((1, D_B)),
            _const_spec((D_B, D_MODEL)),
            _const_spec((D_MODEL, D_MODEL)),
        ],
        out_specs=pl.BlockSpec((1, tm, D_MODEL), lambda b, t: (b, t, 0)),
        out_shape=jax.ShapeDtypeStruct((B, n_rows, D_MODEL), F32),
        compiler_params=_params(("parallel", "parallel")),
        name="outproj",
    )(h, attn, rest, rest, band, lw["wa"], lw["pw"], lw["pb"], lw["ps"], lw["wb"], lw["wo"])
    return out


def _prep_layer_weights(l, tm, norm_gain, w_in, q_norm_gain, k_norm_gain, pool_w, pool_b, pool_scale,
                        w_branch_a, w_branch_b, w_out):
    w = w_in[l]
    o_q, o_ga = 0, D_A + 2 * HEAD_DIM
    o_ub = o_ga + D_A
    o_gb = o_ub + D_B
    o_qi = o_gb + D_B
    o_ki = o_qi + D_A
    o_wi = o_ki + HEAD_DIM
    o_mg = o_wi + N_HEADS
    w_t = jnp.concatenate([w[:, o_q:o_ga], w[:, o_qi:o_mg],
                           jnp.zeros((D_MODEL, N_T_ROWS - (o_ga - o_q) - (o_mg - o_qi)), w.dtype)], axis=1).T
    w_s = jnp.concatenate([w[:, o_ga:o_qi], w[:, o_mg:]], axis=1)
    return {
        "g": norm_gain[l].reshape(1, D_MODEL).astype(F32),
        "wt": w_t.astype(MXU_DTYPE),
        "ws": w_s.astype(MXU_DTYPE),
        "gq": jnp.broadcast_to(q_norm_gain[l].astype(F32)[:, None], (HEAD_DIM, tm)),
        "gk": jnp.broadcast_to(k_norm_gain[l].astype(F32)[:, None], (HEAD_DIM, tm)),
        "wa": w_branch_a[l].astype(MXU_DTYPE),
        "pw": jax.scipy.linalg.block_diag(*[pool_w[l, g] for g in range(len(POOL_WINDOWS))]).astype(MXU_DTYPE),
        "pb": pool_b[l].reshape(1, D_B).astype(F32),
        "ps": pool_scale[l].reshape(1, D_B).astype(F32),
        "wb": w_branch_b[l].astype(MXU_DTYPE),
        "wo": w_out[l].astype(MXU_DTYPE),
    }


@jax.jit
def kernel(x, meta_tokens, norm_gain, w_in, q_norm_gain, k_norm_gain, pool_w, pool_b, pool_scale,
           w_branch_a, w_branch_b, w_out):
    B, S, _ = x.shape
    assert S % LANES == 0, "sequence length must be a multiple of 128"
    k_top = min(TOPK_MAX, S // 4)
    n_rows = LANES + S
    tm = _row_tile(n_rows)

    meta = jnp.broadcast_to(meta_tokens.astype(x.dtype)[None], (B, N_META, D_MODEL))
    h = jnp.concatenate([jnp.zeros((B, PAD, D_MODEL), x.dtype), meta, x], axis=1)

    pos = jnp.maximum(jnp.arange(n_rows, dtype=F32) - PAD, 0.0)
    inv_freq = 1.0 / (ROPE_THETA ** (jnp.arange(0, HEAD_DIM, 2, dtype=F32) / HEAD_DIM))
    ang = inv_freq[:, None] * pos[None, :]
    cos_t, sin_t = jnp.cos(ang), jnp.sin(ang)

    r = jnp.arange(LANES)[:, None]
    c = jnp.arange(2 * LANES)[None, :] - LANES
    band = jnp.stack([((c <= r) & (c > r - w)) for w in POOL_WINDOWS]).astype(MXU_DTYPE)

    for l in range(DEPTH):
        lw = _prep_layer_weights(l, tm, norm_gain, w_in, q_norm_gain, k_norm_gain, pool_w, pool_b, pool_scale,
                                 w_branch_a, w_branch_b, w_out)
        h = _layer(h, lw, cos_t, sin_t, band, k_top=k_top)
    return h[:, LANES:]
```

```python
import functools
import math

import jax
import jax.numpy as jnp
import numpy as np
from jax import lax
from jax.experimental import pallas as pl
from jax.experimental.pallas import tpu as pltpu

D_MODEL = 1024
DEPTH = 4
CHUNK = 64
N_META = 16
N_HEADS = 8
HEAD_DIM = 64
D_A = N_HEADS * HEAD_DIM
D_B = 512
POOL_WINDOWS = (2, 4, 8, 16)
POOL_GROUP = 128
TOPK_MAX = 256
ROPE_THETA = 10000.0
EPS = 1e-6
NEG_INF = -1e30

LANES = 128
PAD = LANES - N_META
KB = LANES
QB = LANES
V_ROWS = HEAD_DIM + 16
N_T_ROWS = 2 * D_A + 3 * HEAD_DIM + 16
D_REST = D_A + D_B + D_B + 2 * D_MODEL
INT_MIN = -2 ** 31
INT_MAX = 2 ** 31 - 1
FLOAT_MIDPOINT_STEPS = 24
BISECT_STEPS_PER_TEST = 2
UNTESTED_BISECT_STEPS = 14
ZERO_TIE_MARGIN = 128
MAX_BISECT_STEPS = FLOAT_MIDPOINT_STEPS + 36
VMEM_LIMIT = 48 * 1024 * 1024

MXU_DTYPE = jnp.bfloat16

F32 = jnp.float32
I32 = jnp.int32


def _dot(a, b):
    return jnp.dot(a, b, preferred_element_type=F32)


def _sortable(bits):
    return bits ^ ((bits >> 31) & 0x7FFFFFFF)


_NEG_KEY = int(_sortable(np.float32(NEG_INF).view(np.int32)))


def _proj_kernel(x_ref, g_ref, wt_ref, ws_ref, gq_ref, gk_ref, cos_ref, sin_ref,
                 qt_ref, qit_ref, wit_ref, kk_ref, vt_ref, rest_ref, *, tm):
    x = x_ref[0]
    ms = jnp.mean(x * x, axis=-1, keepdims=True)
    hn = (x * lax.rsqrt(ms + EPS) * g_ref[...]).astype(MXU_DTYPE)

    for c in range(0, D_REST, 512):
        y = _dot(hn, ws_ref[:, c:c + 512])
        if c in (0, D_A + D_B):
            y = y * jax.nn.sigmoid(y)
        elif c >= D_A + 2 * D_B:
            y = jax.nn.sigmoid(y)
        rest_ref[0, :, c:c + 512] = y.astype(rest_ref.dtype)

    pt = lax.dot_general(wt_ref[...], hn, (((1,), (1,)), ((), ())), preferred_element_type=F32)
    cos = cos_ref[...]
    sin = sin_ref[...]
    half = HEAD_DIM // 2

    def rope(t):
        x1, x2 = t[:half], t[half:]
        return jnp.concatenate([x1 * cos - x2 * sin, x2 * cos + x1 * sin], axis=0)

    def norm(t, g):
        return t * lax.rsqrt(jnp.mean(t * t, axis=0, keepdims=True) + EPS) * g

    gq = gq_ref[...]
    q_scale = (HEAD_DIM ** -0.5) * math.log2(math.e)
    for h in range(N_HEADS):
        r0 = h * HEAD_DIM
        qh = rope(norm(pt[r0:r0 + HEAD_DIM], gq)) * q_scale
        qt_ref[0, r0:r0 + HEAD_DIM, :] = qh.astype(qt_ref.dtype)
        qih = rope(pt[D_A + 2 * HEAD_DIM + r0:D_A + 2 * HEAD_DIM + r0 + HEAD_DIM])
        qit_ref[0, r0:r0 + HEAD_DIM, :] = qih.astype(qit_ref.dtype)

    k_t = rope(norm(pt[D_A:D_A + HEAD_DIM], gk_ref[...]))
    v_t = pt[D_A + HEAD_DIM:D_A + 2 * HEAD_DIM]
    r_ki = 2 * D_A + 2 * HEAD_DIM
    ki_t = rope(pt[r_ki:r_ki + HEAD_DIM])
    wi_t = pt[r_ki + HEAD_DIM:r_ki + HEAD_DIM + N_HEADS]
    wit_ref[0] = wi_t * ((N_HEADS ** -0.5) * (HEAD_DIM ** -0.5))

    kk_t = jnp.concatenate([k_t, ki_t], axis=0)
    for c in range(tm // LANES):
        sl = slice(c * LANES, (c + 1) * LANES)
        kk_ref[0, sl, :] = kk_t[:, sl].T.astype(kk_ref.dtype)
        vt_ref[0, c] = v_t[:, sl].astype(vt_ref.dtype)


def _fold8(x, op):
    parts = [x[r:r + 8] for r in range(0, x.shape[0], 8)]
    while len(parts) > 1:
        nxt = [op(parts[a], parts[a + 1]) for a in range(0, len(parts) - 1, 2)]
        if len(parts) % 2:
            nxt.append(parts[-1])
        parts = nxt
    return parts[0]


def _loop_by_four(n, body, init):
    def quad(t, c):
        for u in range(4):
            c = body(4 * t + u, c)
        return c
    carry = lax.fori_loop(0, n // 4, quad, init)
    done = (n // 4) * 4
    carry = lax.cond(n - done >= 2, lambda c: body(done + 1, body(done, c)), lambda c: c, carry)
    return lax.cond(n % 2 == 1, lambda c: body(n - 1, c), lambda c: c, carry)


def _attn_kernel(qt_ref, qit_next_ref, wit_next_ref, qit_first_ref, wit_first_ref, kk_ref, vt_ref, o_ref,
                 keys2_scr, minmax2_scr, s_scr, acc_scr, *, k_top, n_rows, ks):
    i = pl.program_id(1)
    n_blocks = n_rows // QB
    nkb = ks // KB
    keys_scr = keys2_scr.at[i % 2]
    minmax_scr = minmax2_scr.at[i % 2]
    keys_next_scr = keys2_scr.at[(i + 1) % 2]
    minmax_next_scr = minmax2_scr.at[(i + 1) % 2]
    n_keys = n_rows - PAD
    lane = lax.broadcasted_iota(I32, (1, QB), 1)
    sub = lax.broadcasted_iota(I32, (KB, QB), 0)

    def block_geometry(t):
        q_row = t * QB + lane
        end = jnp.where(q_row < LANES, LANES, ((q_row - LANES) // CHUNK + 1) * CHUNK + LANES)
        return end, t // nkb + 1

    vis_end, ns = block_geometry(i)
    n_unseen = n_rows - ns * ks
    n_vis = vis_end - PAD

    zeros_half = jnp.zeros((HEAD_DIM, N_HEADS * QB), MXU_DTYPE)

    def stack_heads(ref):
        return jnp.concatenate([ref[0, h * HEAD_DIM:(h + 1) * HEAD_DIM, :] for h in range(N_HEADS)], axis=1)

    q_all = jnp.concatenate([stack_heads(qt_ref), zeros_half], axis=0)

    def score_step(j, carry, qi_all, w, end, dst_scr):
        kmin, kmax = carry
        base = pl.multiple_of(j * ks, ks)
        kk_sb = kk_ref[0, pl.ds(base, ks), :]
        s = _dot(kk_sb, qi_all)
        score = jnp.maximum(s[:, :QB], 0.0) * w[0:1, :]
        for h in range(1, N_HEADS):
            score = score + jnp.maximum(s[:, h * QB:(h + 1) * QB], 0.0) * w[h:h + 1, :]
        for r in range(nkb):
            off = base + r * KB
            vis = sub < (end - off)
            sc = jnp.where(vis, score[r * KB:(r + 1) * KB] + 0.0, NEG_INF)
            key = _sortable(lax.bitcast_convert_type(sc, I32))
            dst_scr[pl.ds(pl.multiple_of(off, KB), KB), :] = key
            kmax = jnp.maximum(kmax, _fold8(key, jnp.maximum))
            kmin = jnp.minimum(kmin, _fold8(jnp.where(vis, key, INT_MAX), jnp.minimum))
        return kmin, kmax

    no_minmax = (jnp.full((8, QB), INT_MAX, I32), jnp.full((8, QB), INT_MIN, I32))

    def indexer_operands(qit_ref, wit_ref):
        return jnp.concatenate([zeros_half, stack_heads(qit_ref)], axis=0), wit_ref[0]

    @pl.when(i == 0)
    def _():
        qi_all, w = indexer_operands(qit_first_ref, wit_first_ref)
        kmin, kmax = score_step(0, no_minmax, qi_all, w, vis_end, keys_scr)
        minmax_scr[0:8, :] = kmin
        minmax_scr[8:16, :] = kmax

    keys_scr[0:PAD, :] = jnp.full((PAD, QB), INT_MIN, I32)
    kmin = jnp.min(minmax_scr[0:8, :], axis=0, keepdims=True)
    kmax = jnp.max(minmax_scr[8:16, :], axis=0, keepdims=True)

    @pl.when(ns % 2 == 1)
    def _():
        keys_scr[pl.ds(pl.multiple_of(ns * ks, ks), ks), :] = jnp.full((ks, QB), INT_MIN, I32)

    def reduce_keys(term, op, init):
        def body(t, acc):
            base = pl.multiple_of(t * (2 * ks), 2 * ks)
            for r in range(2 * nkb):
                kb = keys_scr[pl.ds(pl.multiple_of(base + r * KB, KB), KB), :]
                acc = op(acc, _fold8(term(kb), op))
            return acc
        return lax.fori_loop(0, (ns + 1) // 2, body, jnp.full((8, QB), init, I32))

    def count_ge(cand):
        c = reduce_keys(lambda kb: jnp.where(kb >= cand, 1, 0), jnp.add, 0)
        return jnp.sum(c, axis=0, keepdims=True) + jnp.where(_NEG_KEY >= cand, n_unseen, 0)

    few_visible = n_vis < k_top
    lo0 = jnp.where(few_visible, jnp.minimum(kmin, _NEG_KEY), kmin)
    c_lo0 = n_vis + jnp.where(_NEG_KEY >= lo0, n_keys - n_vis, 0)
    top = jnp.maximum(kmax, _NEG_KEY)
    hi0 = jnp.where(top == INT_MAX, INT_MAX, top + 1)

    def finished(lo, hi, c_lo, c_hi):
        return (c_lo == k_top) | (lo + 1 >= hi) | (c_hi == k_top - 1)

    def bisect_cond(st):
        p, lo, hi, c_lo, c_hi = st
        pending = jnp.where(finished(lo, hi, c_lo, c_hi), 0.0, 1.0)
        return (p < MAX_BISECT_STEPS) & (jnp.max(pending) > 0.0)

    def bisect_body(st):
        for _ in range(BISECT_STEPS_PER_TEST):
            st = bisect_step(st)
        return st

    def bisect_step(st):
        p, lo, hi, c_lo, c_hi = st
        done = finished(lo, hi, c_lo, c_hi)
        lo_f = lax.bitcast_convert_type(_sortable(lo), F32)
        hi_f = lax.bitcast_convert_type(_sortable(hi), F32)
        cand_f = _sortable(lax.bitcast_convert_type(lo_f * 0.5 + hi_f * 0.5, I32))
        cand_k = (lo >> 1) + (hi >> 1) + (lo & hi & 1)
        use_f = (cand_f > lo) & (cand_f < hi) & (p < FLOAT_MIDPOINT_STEPS)
        cand = jnp.where(use_f, cand_f, cand_k)
        cand = jnp.where((p == 0) & (lo < 0) & (hi > 0), 0, cand)
        cand = jnp.where((p == 1) & (lo == 0) & (hi > 1) & (c_lo - k_top <= ZERO_TIE_MARGIN), 1, cand)
        cand = jnp.where(few_visible & (p == 0), _NEG_KEY + 1, cand)
        cand = jnp.minimum(jnp.maximum(cand, lo + 1), hi - 1)
        c = count_ge(cand)
        up = (c >= k_top) & jnp.logical_not(done)
        dn = (c < k_top) & jnp.logical_not(done)
        return (p + 1, jnp.where(up, cand, lo), jnp.where(dn, cand, hi),
                jnp.where(up, c, c_lo), jnp.where(dn, c, c_hi))

    st = (jnp.int32(0), lo0, hi0, c_lo0, jnp.zeros((1, QB), I32))
    untested = jnp.where(i * QB > LANES + k_top,
                         jnp.where(i * 3 > n_blocks, UNTESTED_BISECT_STEPS, UNTESTED_BISECT_STEPS - 2), 0)
    untested = untested // BISECT_STEPS_PER_TEST
    st = lax.fori_loop(0, untested, lambda _, s: bisect_body(s), st)
    _, lo, hi, c_lo, c_hi = lax.while_loop(bisect_cond, bisect_body, st)

    below_hi = reduce_keys(lambda kb: jnp.where(kb < hi, kb, INT_MIN), jnp.maximum, INT_MIN)
    below_hi = jnp.max(below_hi, axis=0, keepdims=True)
    below_hi = jnp.where((n_unseen > 0) & (_NEG_KEY < hi), jnp.maximum(below_hi, _NEG_KEY), below_hi)
    exact_cut = c_lo == k_top
    tau = jnp.where(exact_cut | (lo + 1 >= hi), lo, below_hi)

    n_tied_wanted = jnp.where(exact_cut, n_rows, k_top - c_hi).astype(F32)
    row_ge_col = (lax.broadcasted_iota(I32, (KB, KB), 0) >= lax.broadcasted_iota(I32, (KB, KB), 1))
    prefix_ones = jnp.where(row_ge_col, 1.0, 0.0).astype(MXU_DTYPE)

    def logits_body(j, carry):
        m, tied_before = carry
        base = pl.multiple_of(j * ks, ks)
        bias = []
        for r in range(nkb):
            off = base + r * KB
            kb = keys_scr[pl.ds(pl.multiple_of(off, KB), KB), :]
            tied = jnp.where(kb == tau, 1.0, 0.0).astype(MXU_DTYPE)
            rank = _dot(prefix_ones, tied)
            admissible = rank + tied_before <= n_tied_wanted
            tied_before = tied_before + rank[KB - 1:KB, :]
            thr = tau - jnp.where(admissible, 1, 0)
            thr = jnp.where(sub < vis_end - off, thr, INT_MAX)
            bias.append(jnp.where(kb > thr, 0.0, NEG_INF))
        bias = jnp.concatenate(bias, axis=0)
        s = _dot(kk_ref[0, pl.ds(base, ks), :], q_all) + jnp.concatenate([bias] * N_HEADS, axis=1)
        s_scr[pl.ds(base, ks), :] = s
        return jnp.maximum(m, jnp.max(s, axis=0, keepdims=True)), tied_before

    m, _ = _loop_by_four(ns, logits_body,
                         (jnp.full((1, N_HEADS * QB), NEG_INF, F32), jnp.zeros((1, QB), F32)))

    acc_scr[...] = jnp.zeros_like(acc_scr)
    ones_rows = jnp.ones((V_ROWS - HEAD_DIM, ks), MXU_DTYPE)

    def value_body(j, carry):
        base = pl.multiple_of(j * ks, ks)
        p = jnp.exp2(s_scr[pl.ds(base, ks), :] - m).astype(MXU_DTYPE)
        v_ext = jnp.concatenate(
            [jnp.concatenate([vt_ref[0, j * nkb + r] for r in range(nkb)], axis=1), ones_rows], axis=0)
        acc_scr[...] += _dot(v_ext, p)
        return carry

    @pl.when(i + 1 < n_blocks)
    def _():
        end_next, ns_next = block_geometry(i + 1)
        qi_all, w = indexer_operands(qit_next_ref, wit_next_ref)

        def both(j, carry):
            value_body(j, 0)
            return score_step(j, carry, qi_all, w, end_next, keys_next_scr)

        carry = _loop_by_four(ns, both, no_minmax)
        carry = lax.cond(ns_next > ns,
                         lambda c: score_step(ns, c, qi_all, w, end_next, keys_next_scr), lambda c: c, carry)
        minmax_next_scr[0:8, :] = carry[0]
        minmax_next_scr[8:16, :] = carry[1]

    @pl.when(i + 1 == n_blocks)
    def _():
        _loop_by_four(ns, value_body, 0)

    acc = acc_scr[...]
    out_t = acc[:HEAD_DIM] / acc[HEAD_DIM:HEAD_DIM + 1]
    for hp in range(N_HEADS // 2):
        blk = jnp.concatenate([out_t[:, (2 * hp) * QB:(2 * hp + 1) * QB],
                               out_t[:, (2 * hp + 1) * QB:(2 * hp + 2) * QB]], axis=0)
        o_ref[0, :, hp * LANES:(hp + 1) * LANES] = blk.T.astype(o_ref.dtype)


def _out_kernel(x_ref, attn_ref, rest_ref, halo_ref, band_ref, wa_ref, pw_ref, pb_ref, ps_ref, wb_ref,
                wo_ref, o_ref, *, tm):
    t = pl.program_id(1)
    row = t * tm + lax.broadcasted_iota(I32, (tm, LANES), 0)
    pos = row - PAD

    gate_a = rest_ref[0, :, 0:D_A].astype(F32)
    a = attn_ref[0].astype(F32) * gate_a
    y_a = _dot(a.astype(MXU_DTYPE), wa_ref[...])

    u = rest_ref[0, :, D_A:D_A + D_B]
    halo = jnp.where(t > 0, halo_ref[0], jnp.zeros_like(halo_ref[0]))
    u_ext = jnp.concatenate([halo, u], axis=0)
    n_chunks = tm // LANES
    pooled = []
    for g, win in enumerate(POOL_WINDOWS):
        cs = slice(g * POOL_GROUP, (g + 1) * POOL_GROUP)
        windows = jnp.concatenate([u_ext[c * LANES:(c + 2) * LANES, cs] for c in range(n_chunks)], axis=1)
        sums = _dot(band_ref[g], windows)
        window_sum = jnp.concatenate([sums[:, c * LANES:(c + 1) * LANES] for c in range(n_chunks)], axis=0)
        cnt = jnp.clip(pos + 1, 1, win).astype(F32)
        pooled.append((window_sum / cnt - u[:, cs].astype(F32)).astype(MXU_DTYPE))
    mixed = _dot(jnp.concatenate(pooled, axis=1), pw_ref[...])
    mixed = (mixed + pb_ref[...]) * ps_ref[...]
    gate_b = rest_ref[0, :, D_A + D_B:D_A + 2 * D_B].astype(F32)
    z_b = mixed * gate_b
    y_b = _dot(z_b.astype(MXU_DTYPE), wb_ref[...])

    c0 = D_A + 2 * D_B
    g_a = rest_ref[0, :, c0:c0 + D_MODEL].astype(F32)
    g_b = rest_ref[0, :, c0 + D_MODEL:c0 + 2 * D_MODEL].astype(F32)
    mix = g_a * y_a + g_b * y_b
    out = x_ref[0] + _dot(mix.astype(MXU_DTYPE), wo_ref[...])
    row_full = t * tm + lax.broadcasted_iota(I32, (tm, D_MODEL), 0)
    o_ref[0] = jnp.where(row_full >= PAD, out, 0.0)


def _row_tile(n_rows):
    for tm in (384, 256, 128):
        if n_rows % tm == 0:
            return tm
    raise ValueError(f"sequence rows {n_rows} must be a multiple of 128")


def _key_step(n_rows):
    return 3 * KB if n_rows % (3 * KB) == 0 else KB


def _params(sem):
    return pltpu.CompilerParams(dimension_semantics=sem, vmem_limit_bytes=VMEM_LIMIT)


def _const_spec(shape):
    return pl.BlockSpec(shape, lambda b, t: (0,) * len(shape))


def _layer(h, lw, cos_t, sin_t, band, *, k_top):
    B, n_rows, _ = h.shape
    tm = _row_tile(n_rows)
    nt = n_rows // tm
    nblk = n_rows // LANES
    act = MXU_DTYPE

    qt, qit, wit, kk, vt, rest = pl.pallas_call(
        functools.partial(_proj_kernel, tm=tm),
        grid=(B, nt),
        in_specs=[
            pl.BlockSpec((1, tm, D_MODEL), lambda b, t: (b, t, 0)),
            _const_spec((1, D_MODEL)),
            _const_spec((N_T_ROWS, D_MODEL)),
            _const_spec((D_MODEL, D_REST)),
            _const_spec((HEAD_DIM, tm)),
            _const_spec((HEAD_DIM, tm)),
            pl.BlockSpec((HEAD_DIM // 2, tm), lambda b, t: (0, t)),
            pl.BlockSpec((HEAD_DIM // 2, tm), lambda b, t: (0, t)),
        ],
        out_specs=[
            pl.BlockSpec((1, D_A, tm), lambda b, t: (b, 0, t)),
            pl.BlockSpec((1, D_A, tm), lambda b, t: (b, 0, t)),
            pl.BlockSpec((1, N_HEADS, tm), lambda b, t: (b, 0, t)),
            pl.BlockSpec((1, tm, LANES), lambda b, t: (b, t, 0)),
            pl.BlockSpec((1, tm // LANES, HEAD_DIM, LANES), lambda b, t: (b, t, 0, 0)),
            pl.BlockSpec((1, tm, D_REST), lambda b, t: (b, t, 0)),
        ],
        out_shape=[
            jax.ShapeDtypeStruct((B, D_A, n_rows), act),
            jax.ShapeDtypeStruct((B, D_A, n_rows), act),
            jax.ShapeDtypeStruct((B, N_HEADS, n_rows), F32),
            jax.ShapeDtypeStruct((B, n_rows, LANES), act),
            jax.ShapeDtypeStruct((B, nblk, HEAD_DIM, LANES), act),
            jax.ShapeDtypeStruct((B, n_rows, D_REST), act),
        ],
        compiler_params=_params(("parallel", "parallel")),
        name="proj",
    )(h, lw["g"], lw["wt"], lw["ws"], lw["gq"], lw["gk"], cos_t, sin_t)

    assert n_rows - PAD >= k_top
    ks = _key_step(n_rows)
    attn = pl.pallas_call(
        functools.partial(_attn_kernel, k_top=k_top, n_rows=n_rows, ks=ks),
        grid=(B, nblk),
        in_specs=[
            pl.BlockSpec((1, D_A, QB), lambda b, i: (b, 0, i)),
            pl.BlockSpec((1, D_A, QB), lambda b, i: (b, 0, jnp.minimum(i + 1, nblk - 1))),
            pl.BlockSpec((1, N_HEADS, QB), lambda b, i: (b, 0, jnp.minimum(i + 1, nblk - 1))),
            pl.BlockSpec((1, D_A, QB), lambda b, i: (b, 0, 0)),
            pl.BlockSpec((1, N_HEADS, QB), lambda b, i: (b, 0, 0)),
            pl.BlockSpec((1, n_rows, LANES), lambda b, i: (b, 0, 0)),
            pl.BlockSpec((1, nblk, HEAD_DIM, LANES), lambda b, i: (b, 0, 0, 0)),
        ],
        out_specs=pl.BlockSpec((1, QB, D_A), lambda b, i: (b, i, 0)),
        out_shape=jax.ShapeDtypeStruct((B, n_rows, D_A), act),
        scratch_shapes=[
            pltpu.VMEM((2, n_rows + ks, QB), I32),
            pltpu.VMEM((2, 16, QB), I32),
            pltpu.VMEM((n_rows, N_HEADS * QB), F32),
            pltpu.VMEM((V_ROWS, N_HEADS * QB), F32),
        ],
        compiler_params=_params(("arbitrary", "arbitrary")),
        name="attn",
    )(qt, qit, wit, qit, wit, kk, vt)

    halo_blocks = tm // LANES
    out = pl.pallas_call(
        functools.partial(_out_kernel, tm=tm),
        grid=(B, nt),
        in_specs=[
            pl.BlockSpec((1, tm, D_MODEL), lambda b, t: (b, t, 0)),
            pl.BlockSpec((1, tm, D_A), lambda b, t: (b, t, 0)),
            pl.BlockSpec((1, tm, D_REST), lambda b, t: (b, t, 0)),
            pl.BlockSpec((1, LANES, D_B), lambda b, t: (b, jnp.maximum(t * halo_blocks - 1, 0), 1)),
            _const_spec((len(POOL_WINDOWS), LANES, 2 * LANES)),
            _const_spec((D_A, D_MODEL)),
            _const_spec((D_B, D_B)),
            _const_spec((1, D_B)),
            _const_spec((1, D_B)),
            _const_spec((D_B, D_MODEL)),
            _const_spec((D_MODEL, D_MODEL)),
        ],
        out_specs=pl.BlockSpec((1, tm, D_MODEL), lambda b, t: (b, t, 0)),
        out_shape=jax.ShapeDtypeStruct((B, n_rows, D_MODEL), F32),
        compiler_params=_params(("parallel", "parallel")),
        name="outproj",
    )(h, attn, rest, rest, band, lw["wa"], lw["pw"], lw["pb"], lw["ps"], lw["wb"], lw["wo"])
    return out


def _prep_layer_weights(l, tm, norm_gain, w_in, q_norm_gain, k_norm_gain, pool_w, pool_b, pool_scale,
                        w_branch_a, w_branch_b, w_out):
    w = w_in[l]
    o_q, o_ga = 0, D_A + 2 * HEAD_DIM
    o_ub = o_ga + D_A
    o_gb = o_ub + D_B
    o_qi = o_gb + D_B
    o_ki = o_qi + D_A
    o_wi = o_ki + HEAD_DIM
    o_mg = o_wi + N_HEADS
    w_t = jnp.concatenate([w[:, o_q:o_ga], w[:, o_qi:o_mg],
                           jnp.zeros((D_MODEL, N_T_ROWS - (o_ga - o_q) - (o_mg - o_qi)), w.dtype)], axis=1).T
    w_s = jnp.concatenate([w[:, o_ga:o_qi], w[:, o_mg:]], axis=1)
    return {
        "g": norm_gain[l].reshape(1, D_MODEL).astype(F32),
        "wt": w_t.astype(MXU_DTYPE),
        "ws": w_s.astype(MXU_DTYPE),
        "gq": jnp.broadcast_to(q_norm_gain[l].astype(F32)[:, None], (HEAD_DIM, tm)),
        "gk": jnp.broadcast_to(k_norm_gain[l].astype(F32)[:, None], (HEAD_DIM, tm)),
        "wa": w_branch_a[l].astype(MXU_DTYPE),
        "pw": jax.scipy.linalg.block_diag(*[pool_w[l, g] for g in range(len(POOL_WINDOWS))]).astype(MXU_DTYPE),
        "pb": pool_b[l].reshape(1, D_B).astype(F32),
        "ps": pool_scale[l].reshape(1, D_B).astype(F32),
        "wb": w_branch_b[l].astype(MXU_DTYPE),
        "wo": w_out[l].astype(MXU_DTYPE),
    }


@jax.jit
def kernel(x, meta_tokens, norm_gain, w_in, q_norm_gain, k_norm_gain, pool_w, pool_b, pool_scale,
           w_branch_a, w_branch_b, w_out):
    B, S, _ = x.shape
    assert S % LANES == 0, "sequence length must be a multiple of 128"
    k_top = min(TOPK_MAX, S // 4)
    n_rows = LANES + S
    tm = _row_tile(n_rows)

    meta = jnp.broadcast_to(meta_tokens.astype(x.dtype)[None], (B, N_META, D_MODEL))
    h = jnp.concatenate([jnp.zeros((B, PAD, D_MODEL), x.dtype), meta, x], axis=1)

    pos = jnp.maximum(jnp.arange(n_rows, dtype=F32) - PAD, 0.0)
    inv_freq = 1.0 / (ROPE_THETA ** (jnp.arange(0, HEAD_DIM, 2, dtype=F32) / HEAD_DIM))
    ang = inv_freq[:, None] * pos[None, :]
    cos_t, sin_t = jnp.cos(ang), jnp.sin(ang)

    r = jnp.arange(LANES)[:, None]
    c = jnp.arange(2 * LANES)[None, :] - LANES
    band = jnp.stack([((c <= r) & (c > r - w)) for w in POOL_WINDOWS]).astype(MXU_DTYPE)

    for l in range(DEPTH):
        lw = _prep_layer_weights(l, tm, norm_gain, w_in, q_norm_gain, k_norm_gain, pool_w, pool_b, pool_scale,
                                 w_branch_a, w_branch_b, w_out)
        h = _layer(h, lw, cos_t, sin_t, band, k_top=k_top)
    return h[:, LANES:]
```
